```python
import math
import jax, jax.numpy as jnp
from jax import lax
import numpy as np

D_MODEL = 1024
BATCH = 4
SEQ = 4096
DEPTH = 1

CTX_LEN = 256
GRID_W = 64
CHUNK = 128
RET_HEADS = 4
RET_QK_DIM = 128
RET_V_DIM = 128
RET_WIDTH = RET_HEADS * RET_V_DIM
GMLP_GROUPS = 4
GMLP_GROUP_DIM = 128
GMLP_WIDTH = GMLP_GROUPS * GMLP_GROUP_DIM
MIX_WIDTH = RET_WIDTH + GMLP_WIDTH
IN_COLS = 4 * RET_WIDTH + 2 * GMLP_WIDTH
D_FF = 2816
N_MOD = 9
ROPE_BASE = 10000.0
EPS = 1e-6

kernel_name = "hybrid_retention_gmlp_macaron_dit"


def rmsnorm(x, g):
    xf = x.astype(jnp.float32)
    y = xf * lax.rsqrt(jnp.mean(xf * xf, axis=-1, keepdims=True) + EPS)
    return (y * g.astype(jnp.float32)).astype(x.dtype)


def modulate(h, shift, scale):
    return h * (1 + scale) + shift


def swiglu(h, w1, w2):
    gate, up = jnp.split(h @ w1, 2, axis=-1)
    return (jax.nn.silu(gate) * up) @ w2


def to_heads(t, n_heads):
    b, n, _ = t.shape
    return t.reshape(b, n, n_heads, -1).transpose(0, 2, 1, 3)


def rope_2d(t, row, col):
    half = t.shape[-1] // 2
    n_freq = half // 2
    freqs = ROPE_BASE ** (-jnp.arange(n_freq, dtype=jnp.float32) / n_freq)
    ang = jnp.concatenate([row[:, None] * freqs, col[:, None] * freqs], axis=-1)
    cos, sin = jnp.cos(ang), jnp.sin(ang)
    t1, t2 = t[..., :half], t[..., half:]
    return jnp.concatenate([t1 * cos - t2 * sin, t1 * sin + t2 * cos], axis=-1)


def retention_scan(q, k, v, log_gamma, s0):
    b, h, n, dk = q.shape
    dv = v.shape[-1]
    nc = n // CHUNK
    idx = jnp.arange(CHUNK, dtype=jnp.float32)
    diff = idx[:, None] - idx[None, :]
    lower = diff >= 0
    dmask = jnp.where(lower[None], jnp.exp(jnp.where(lower, diff, 0.0)[None] * log_gamma[:, None, None]), 0.0)
    xi = jnp.exp((idx + 1.0)[None] * log_gamma[:, None])
    zeta = jnp.exp((CHUNK - 1.0 - idx)[None] * log_gamma[:, None])
    chunk_decay = jnp.exp(CHUNK * log_gamma)

    def to_chunks(t):
        return t.reshape(b, h, nc, CHUNK, -1).transpose(2, 0, 1, 3, 4)

    def step(s, qkv):
        qc, kc, vc = qkv
        scores = jnp.einsum('bhid,bhjd->bhij', qc, kc) * dmask
        o = (jnp.einsum('bhij,bhjv->bhiv', scores, vc)
             + jnp.einsum('bhid,bhdv->bhiv', qc * xi[:, :, None], s))
        s_new = chunk_decay[:, None, None] * s + jnp.einsum('bhjd,bhjv->bhdv', kc * zeta[:, :, None], vc)
        return s_new, o

    s_fin, o = lax.scan(step, s0, (to_chunks(q), to_chunks(k), to_chunks(v)))
    o = o.transpose(1, 2, 0, 3, 4).reshape(b, h, n, dv)
    return o, s_fin


def bidir_retention(q, k, v, lg_fwd, lg_bwd, s0_fwd, s0_bwd):
    flip = lambda t: t[:, :, ::-1]
    o_f, s_f = retention_scan(q, k, v, lg_fwd, s0_fwd)
    o_b, s_b = retention_scan(flip(q), flip(k), flip(v), lg_bwd, s0_bwd)
    return o_f + flip(o_b), s_f, s_b


def retention_readout(y, gate):
    y = y * lax.rsqrt(jnp.mean(y * y, axis=-1, keepdims=True) + EPS)
    b, h, n, dv = y.shape
    y = y.transpose(0, 2, 1, 3).reshape(b, n, h * dv).astype(gate.dtype)
    return y * jax.nn.silu(gate)


def chunk_gmlp(u, v, w_s, b_s):
    b, n, _ = u.shape
    nc = n // CHUNK
    vg = v.reshape(b, nc, CHUNK, GMLP_GROUPS, GMLP_GROUP_DIM).astype(jnp.float32)
    mu = jnp.mean(vg, axis=-1, keepdims=True)
    var = jnp.mean(jnp.square(vg - mu), axis=-1, keepdims=True)
    vn = ((vg - mu) * lax.rsqrt(var + EPS)).astype(u.dtype)
    mixed = jnp.einsum('gpq,bcqgd->bcpgd', w_s, vn) + b_s.T[None, None, :, :, None]
    ug = u.reshape(b, nc, CHUNK, GMLP_GROUPS, GMLP_GROUP_DIM)
    return (ug * mixed).reshape(b, n, GMLP_WIDTH)


def split_proj(p):
    r, g = RET_WIDTH, GMLP_WIDTH
    return jnp.split(p, [r, 2 * r, 3 * r, 4 * r, 4 * r + g], axis=-1)


def setup_inputs(seed: int = 0) -> dict:
    key = jax.random.key(seed)
    ks = jax.random.split(key, 20)
    f32 = jnp.float32
    nrm = lambda k, s, sc: jax.random.normal(k, s, f32) * sc
    base_rate = np.log(-np.log(1.0 - 2.0 ** (-5.0 - np.arange(RET_HEADS)))).astype(np.float32)
    ret_decay = jnp.asarray(base_rate)[None, None, :] + nrm(ks[12], (DEPTH, 2, RET_HEADS), 0.05)
    return {
        "x": nrm(ks[0], (BATCH, SEQ, D_MODEL), 1.0),
        "c": nrm(ks[1], (BATCH, D_MODEL), 1.0),
        "ctx": nrm(ks[2], (BATCH, CTX_LEN, D_MODEL), 1.0),
        "c_ctx": nrm(ks[3], (D_MODEL,), 1.0),
        "w_ada": nrm(ks[4], (DEPTH, D_MODEL, N_MOD * D_MODEL), 0.02),
        "b_ada": nrm(ks[5], (DEPTH, N_MOD * D_MODEL), 0.02),
        "norm_g": 1.0 + nrm(ks[6], (DEPTH, 3, D_MODEL), 0.02),
        "ffn1_w1": nrm(ks[7], (DEPTH, D_MODEL, 2 * D_FF), D_MODEL ** -0.5),
        "ffn1_w2": nrm(ks[8], (DEPTH, D_FF, D_MODEL), D_FF ** -0.5),
        "w_in": nrm(ks[9], (DEPTH, D_MODEL, IN_COLS), D_MODEL ** -0.5),
        "ret_decay": ret_decay,
        "gmlp_ws": nrm(ks[10], (DEPTH, GMLP_GROUPS, CHUNK, CHUNK), 0.5 * CHUNK ** -0.5),
        "gmlp_bs": 1.0 + nrm(ks[11], (DEPTH, GMLP_GROUPS, CHUNK), 0.02),
        "w_out": nrm(ks[13], (DEPTH, MIX_WIDTH, D_MODEL), MIX_WIDTH ** -0.5),
        "ffn2_w1": nrm(ks[14], (DEPTH, D_MODEL, 2 * D_FF), D_MODEL ** -0.5),
        "ffn2_w2": nrm(ks[15], (DEPTH, D_FF, D_MODEL), D_FF ** -0.5),
        "final_g": 1.0 + nrm(ks[16], (D_MODEL,), 0.02),
    }


def reference(x, c, ctx, c_ctx, w_ada, b_ada, norm_g, ffn1_w1, ffn1_w2, w_in, ret_decay,
              gmlp_ws, gmlp_bs, w_out, ffn2_w1, ffn2_w2, final_g):
    b, n, _ = x.shape
    rows = n // GRID_W
    row = jnp.repeat(jnp.arange(rows, dtype=jnp.float32), GRID_W)
    col = jnp.tile(jnp.arange(GRID_W, dtype=jnp.float32), rows)
    k_scale = RET_QK_DIM ** -0.5

    for l in range(DEPTH):
        last = l == DEPTH - 1
        m_x = [t[:, None, :] for t in jnp.split(jax.nn.silu(c) @ w_ada[l] + b_ada[l], N_MOD, axis=-1)]
        m_c = [t[None, None, :] for t in jnp.split(jax.nn.silu(c_ctx) @ w_ada[l] + b_ada[l], N_MOD, axis=-1)]

        x = x + 0.5 * m_x[2] * swiglu(modulate(rmsnorm(x, norm_g[l, 0]), m_x[0], m_x[1]), ffn1_w1[l], ffn1_w2[l])
        ctx = ctx + 0.5 * m_c[2] * swiglu(modulate(rmsnorm(ctx, norm_g[l, 0]), m_c[0], m_c[1]), ffn1_w1[l], ffn1_w2[l])

        px = modulate(rmsnorm(x, norm_g[l, 1]), m_x[3], m_x[4]) @ w_in[l]
        pc = modulate(rmsnorm(ctx, norm_g[l, 1]), m_c[3], m_c[4]) @ w_in[l]
        qx, kx, vx, gx, ux, vgx = split_proj(px)
        qc, kc, vc, gc, uc, vgc = split_proj(pc)

        lg_fwd = -jnp.exp(ret_decay[l, 0].astype(jnp.float32))
        lg_bwd = -jnp.exp(ret_decay[l, 1].astype(jnp.float32))

        qch = to_heads(qc, RET_HEADS).astype(jnp.float32)
        kch = to_heads(kc, RET_HEADS).astype(jnp.float32) * k_scale
        vch = to_heads(vc, RET_HEADS).astype(jnp.float32)
        s_zero = jnp.zeros((b, RET_HEADS, RET_QK_DIM, RET_V_DIM), jnp.float32)
        yc, s_fwd_c, s_bwd_c = bidir_retention(qch, kch, vch, lg_fwd, lg_bwd, s_zero, s_zero)

        qxh = rope_2d(to_heads(qx, RET_HEADS).astype(jnp.float32), row, col)
        kxh = rope_2d(to_heads(kx, RET_HEADS).astype(jnp.float32), row, col) * k_scale
        vxh = to_heads(vx, RET_HEADS).astype(jnp.float32)
        yx, _, _ = bidir_retention(qxh, kxh, vxh, lg_fwd, lg_bwd, s_fwd_c, s_bwd_c)
        ret_x = retention_readout(yx, gx)

        gm_x = chunk_gmlp(jax.nn.gelu(ux), jax.nn.gelu(vgx), gmlp_ws[l], gmlp_bs[l])

        x = x + m_x[5] * (jnp.concatenate([ret_x, gm_x], axis=-1) @ w_out[l])
        if not last:
            ret_c = retention_readout(yc, gc)
            gm_c = chunk_gmlp(jax.nn.gelu(uc), jax.nn.gelu(vgc), gmlp_ws[l], gmlp_bs[l])
            ctx = ctx + m_c[5] * (jnp.concatenate([ret_c, gm_c], axis=-1) @ w_out[l])

        x = x + 0.5 * m_x[8] * swiglu(modulate(rmsnorm(x, norm_g[l, 2]), m_x[6], m_x[7]), ffn2_w1[l], ffn2_w2[l])
        if not last:
            ctx = ctx + 0.5 * m_c[8] * swiglu(modulate(rmsnorm(ctx, norm_g[l, 2]), m_c[6], m_c[7]), ffn2_w1[l], ffn2_w2[l])

    return rmsnorm(x, final_g)
```

```python
import functools
import math

import jax
import jax.numpy as jnp
from jax import lax
from jax.experimental import pallas as pl
from jax.experimental.pallas import tpu as pltpu

F32 = jnp.float32
BF16 = jnp.bfloat16

D_MODEL = 1024
CHUNK = 128
HEADS = 4
HEAD_DIM = 128
GROUPS = 4
GROUP_DIM = 128
RET_WIDTH = HEADS * HEAD_DIM
GMLP_WIDTH = GROUPS * GROUP_DIM
D_FF = 2816
N_MOD = 9
GRID_W = 64
ROPE_BASE = 10000.0
EPS = 1e-6

ROW_TILE = 1024
FF_TILE = 256
MOD_ROWS = 8
VMEM_LIMIT = 56 * 1024 * 1024


def _rmsnorm(x, g):
    return x * lax.rsqrt(jnp.mean(x * x, axis=-1, keepdims=True) + EPS) * g


def _mod_slice(mod_ref, k):
    return mod_ref[0, :, k * D_MODEL:(k + 1) * D_MODEL]


def _dot(a, b):
    return jnp.dot(a, b, preferred_element_type=F32)


def _mods_kernel(c_ref, w_ref, b_ref, o_ref):
    s = jax.nn.silu(c_ref[...]).astype(BF16)
    o_ref[...] = _dot(s, w_ref[...].astype(BF16)) + b_ref[...]


def _mods_call(cc, w_ada, b_ada):
    n_out = w_ada.shape[1]
    tn = D_MODEL
    return pl.pallas_call(
        _mods_kernel,
        grid=(n_out // tn,),
        in_specs=[
            pl.BlockSpec((MOD_ROWS, D_MODEL), lambda j: (0, 0)),
            pl.BlockSpec((D_MODEL, tn), lambda j: (0, j)),
            pl.BlockSpec((1, tn), lambda j: (0, j)),
        ],
        out_specs=pl.BlockSpec((MOD_ROWS, tn), lambda j: (0, j)),
        out_shape=jax.ShapeDtypeStruct((MOD_ROWS, n_out), F32),
        compiler_params=pltpu.CompilerParams(
            dimension_semantics=("arbitrary",), vmem_limit_bytes=VMEM_LIMIT),
        name="mods",
    )(cc, w_ada, b_ada)


def _ffn_kernel(x_ref, mod_ref, g_ref, w1g_ref, w1u_ref, w2_ref, *rest, k0, final_norm):
    if final_norm:
        fg_ref, o_ref, xn_ref, acc_ref = rest
    else:
        o_ref, xn_ref, acc_ref = rest
    j = pl.program_id(1)

    @pl.when(j == 0)
    def _():
        y = _rmsnorm(x_ref[...], g_ref[...])
        xn_ref[...] = (y * (1 + _mod_slice(mod_ref, k0 + 1)) + _mod_slice(mod_ref, k0)).astype(BF16)
        acc_ref[...] = jnp.zeros_like(acc_ref)

    xn = xn_ref[...]
    gate = _dot(xn, w1g_ref[...])
    up = _dot(xn, w1u_ref[...])
    h = (jax.nn.silu(gate) * up).astype(BF16)
    acc_ref[...] += _dot(h, w2_ref[...])

    @pl.when(j == pl.num_programs(1) - 1)
    def _():
        out = x_ref[...] + 0.5 * _mod_slice(mod_ref, k0 + 2) * acc_ref[...]
        if final_norm:
            out = _rmsnorm(out, fg_ref[...])
        o_ref[...] = out


def _ffn_call(xr, mods3, g, w1, w2, *, k0, tiles_per_batch, mod_row0, final_g=None):
    rows = xr.shape[0]
    tm, tf = ROW_TILE, FF_TILE
    n_ff = D_FF // tf
    mod_map = lambda i, j: (mod_row0 + i // tiles_per_batch, 0, 0)
    in_specs = [
        pl.BlockSpec((tm, D_MODEL), lambda i, j: (i, 0)),
        pl.BlockSpec((1, 1, N_MOD * D_MODEL), mod_map),
        pl.BlockSpec((1, D_MODEL), lambda i, j: (0, 0)),
        pl.BlockSpec((D_MODEL, tf), lambda i, j: (0, j)),
        pl.BlockSpec((D_MODEL, tf), lambda i, j: (0, j + n_ff)),
        pl.BlockSpec((tf, D_MODEL), lambda i, j: (j, 0)),
    ]
    args = [xr, mods3, g, w1, w1, w2]
    if final_g is not None:
        in_specs.append(pl.BlockSpec((1, D_MODEL), lambda i, j: (0, 0)))
        args.append(final_g)
    return pl.pallas_call(
        functools.partial(_ffn_kernel, k0=k0, final_norm=final_g is not None),
        grid=(rows // tm, n_ff),
        in_specs=in_specs,
        out_specs=pl.BlockSpec((tm, D_MODEL), lambda i, j: (i, 0)),
        out_shape=jax.ShapeDtypeStruct((rows, D_MODEL), F32),
        scratch_shapes=[pltpu.VMEM((tm, D_MODEL), BF16), pltpu.VMEM((tm, D_MODEL), F32)],
        compiler_params=pltpu.CompilerParams(
            dimension_semantics=("parallel", "arbitrary"), vmem_limit_bytes=VMEM_LIMIT),
        name="ffn_final" if final_g is not None else "ffn",
    )(*args)


_PROJ_COL = {"q": 0, "k": 1, "v": 2, "g": 3, "u": 4, "vg": 5}


def _proj_kernel(x_ref, mod_ref, g_ref, w_ref, *rest, outs, rope):
    if rope:
        cs_ref, sn_ref = rest[:2]
        rest = rest[2:]
    o_refs = dict(zip(outs, rest))
    y = _rmsnorm(x_ref[...], g_ref[...])
    xn = (y * (1 + _mod_slice(mod_ref, 4)) + _mod_slice(mod_ref, 3)).astype(BF16)
    k_scale = HEAD_DIM ** -0.5
    for name in outs:
        col = _PROJ_COL[name] * RET_WIDTH
        p = _dot(xn, w_ref[:, col:col + RET_WIDTH])
        o_ref = o_refs[name]
        if name in ("q", "k", "v", "g"):
            for h in range(HEADS):
                t = p[:, h * HEAD_DIM:(h + 1) * HEAD_DIM]
                if name in ("q", "k") and rope:
                    t = t * cs_ref[...] + pltpu.roll(t, HEAD_DIM // 2, 1) * sn_ref[...]
                if name == "k":
                    t = t * k_scale
                if name == "g":
                    t = jax.nn.silu(t)
                o_ref[h] = t.astype(BF16)
        elif name == "u":
            o_ref[...] = jax.nn.gelu(p).astype(BF16)
        else:
            gv = jax.nn.gelu(p)
            for gi in range(GROUPS):
                t = gv[:, gi * GROUP_DIM:(gi + 1) * GROUP_DIM]
                mu = jnp.mean(t, axis=-1, keepdims=True)
                d = t - mu
                var = jnp.mean(d * d, axis=-1, keepdims=True)
                o_ref[:, gi * GROUP_DIM:(gi + 1) * GROUP_DIM] = (d * lax.rsqrt(var + EPS)).astype(BF16)


def _proj_call(xr, mods3, g, w_in, *, outs, tiles_per_batch, mod_row0, rope_tabs=None):
    rows = xr.shape[0]
    tm = ROW_TILE
    mod_map = lambda i: (mod_row0 + i // tiles_per_batch, 0, 0)
    in_specs = [
        pl.BlockSpec((tm, D_MODEL), lambda i: (i, 0)),
        pl.BlockSpec((1, 1, N_MOD * D_MODEL), mod_map),
        pl.BlockSpec((1, D_MODEL), lambda i: (0, 0)),
        pl.BlockSpec(w_in.shape, lambda i: (0, 0)),
    ]
    args = [xr, mods3, g, w_in]
    if rope_tabs is not None:
        tab_map = lambda i: (i % tiles_per_batch, 0)
        in_specs += [pl.BlockSpec((tm, HEAD_DIM), tab_map)] * 2
        args += list(rope_tabs)
    out_specs, out_shape = [], []
    for name in outs:
        if name in ("q", "k", "v", "g"):
            out_specs.append(pl.BlockSpec((HEADS, tm, HEAD_DIM), lambda i: (0, i, 0)))
            out_shape.append(jax.ShapeDtypeStruct((HEADS, rows, HEAD_DIM), BF16))
        else:
            out_specs.append(pl.BlockSpec((tm, GMLP_WIDTH), lambda i: (i, 0)))
            out_shape.append(jax.ShapeDtypeStruct((rows, GMLP_WIDTH), BF16))
    return pl.pallas_call(
        functools.partial(_proj_kernel, outs=outs, rope=rope_tabs is not None),
        grid=(rows // tm,),
        in_specs=in_specs,
        out_specs=out_specs,
        out_shape=out_shape,
        compiler_params=pltpu.CompilerParams(
            dimension_semantics=("parallel",), vmem_limit_bytes=VMEM_LIMIT),
        name="proj_x" if rope_tabs is not None else "proj_ctx",
    )(*args)


def _kv_outer(kz, v):
    return lax.dot_general(kz, v, (((0,), (0,)), ((), ())), preferred_element_type=F32)


def _ret_kernel(rd_ref, kc_ref, vc_ref, q_ref, k_ref, v_ref, sg_ref, o_ref, sb_ref, *, n_chunks, n_ctx_chunks):
    L = CHUNK
    lg_f = -jnp.exp(rd_ref[0, 0, 0:1, :])
    lg_b = -jnp.exp(rd_ref[1, 0, 0:1, :])
    ii = lax.broadcasted_iota(jnp.int32, (L, L), 0).astype(F32)
    jj = lax.broadcasted_iota(jnp.int32, (L, L), 1).astype(F32)
    diff = ii - jj
    lower = diff >= 0
    upper = diff <= 0
    mask = (jnp.where(lower, jnp.exp(jnp.where(lower, diff, 0.0) * lg_f), 0.0)
            + jnp.where(upper, jnp.exp(jnp.where(upper, -diff, 0.0) * lg_b), 0.0))
    xi_f = jnp.exp((ii + 1.0) * lg_f)
    ze_f = jnp.exp((L - 1.0 - ii) * lg_f)
    xi_b = jnp.exp((L - ii) * lg_b)
    ze_b = jnp.exp(ii * lg_b)
    cd_f = jnp.exp(L * lg_f)
    cd_b = jnp.exp(L * lg_b)

    def kz(kref, c, ze):
        return (kref[0, pl.ds(c * L, L), :].astype(F32) * ze).astype(BF16)

    zero = jnp.zeros((HEAD_DIM, HEAD_DIM), F32)
    s_f = zero
    for c in range(n_ctx_chunks):
        s_f = cd_f * s_f + _kv_outer(kz(kc_ref, c, ze_f), vc_ref[0, c * L:(c + 1) * L, :])
    s_b = zero
    for c in reversed(range(n_ctx_chunks)):
        s_b = cd_b * s_b + _kv_outer(kz(kc_ref, c, ze_b), vc_ref[0, c * L:(c + 1) * L, :])

    def bwd_body(t, s):
        c = n_chunks - 1 - t
        sb_ref[c] = s.astype(BF16)
        return cd_b * s + _kv_outer(kz(k_ref, c, ze_b), v_ref[0, pl.ds(c * L, L), :])

    lax.fori_loop(0, n_chunks, bwd_body, s_b)

    def fwd_body(c, s):
        rows = pl.ds(c * L, L)
        q = q_ref[0, rows, :]
        k = k_ref[0, rows, :]
        v = v_ref[0, rows, :]
        qf = q.astype(F32)
        scores = lax.dot_general(q, k, (((1,), (1,)), ((), ())), preferred_element_type=F32)
        o = (_dot((scores * mask).astype(BF16), v)
             + _dot((qf * xi_f).astype(BF16), s.astype(BF16))
             + _dot((qf * xi_b).astype(BF16), sb_ref[c]))
        y = o * lax.rsqrt(jnp.mean(o * o, axis=-1, keepdims=True) + EPS)
        o_ref[0, rows, :] = (y * sg_ref[0, rows, :].astype(F32)).astype(BF16)
        return cd_f * s + _kv_outer((k.astype(F32) * ze_f).astype(BF16), v)

    lax.fori_loop(0, n_chunks, fwd_body, s_f)


def _ret_call(rd, kc, vc, q, k, v, sg, *, batch, seq, ctx_len):
    n_chunks = seq // CHUNK
    n_ctx_chunks = ctx_len // CHUNK
    x_spec = pl.BlockSpec((1, seq, HEAD_DIM), lambda b, h: (h, b, 0))
    c_spec = pl.BlockSpec((1, ctx_len, HEAD_DIM), lambda b, h: (h, b, 0))
    return pl.pallas_call(
        functools.partial(_ret_kernel, n_chunks=n_chunks, n_ctx_chunks=n_ctx_chunks),
        grid=(batch, HEADS),
        in_specs=[pl.BlockSpec((2, 1, 8, HEAD_DIM), lambda b, h: (0, h, 0, 0)),
                  c_spec, c_spec, x_spec, x_spec, x_spec, x_spec],
        out_specs=x_spec,
        out_shape=jax.ShapeDtypeStruct((HEADS, batch * seq, HEAD_DIM), BF16),
        scratch_shapes=[pltpu.VMEM((n_chunks, HEAD_DIM, HEAD_DIM), BF16)],
        compiler_params=pltpu.CompilerParams(
            dimension_semantics=("parallel", "parallel"), vmem_limit_bytes=VMEM_LIMIT),
        name="retention",
    )(rd, kc, vc, q, k, v, sg)


def _mix_kernel(x_ref, mod_ref, ret_ref, gu_ref, vn_ref, ws_ref, bs_ref, wo_ref, o_ref, mix_ref):
    tm = x_ref.shape[0]
    for h in range(HEADS):
        mix_ref[:, h * HEAD_DIM:(h + 1) * HEAD_DIM] = ret_ref[h]
    for c in range(tm // CHUNK):
        rows = slice(c * CHUNK, (c + 1) * CHUNK)
        for gi in range(GROUPS):
            cols = slice(gi * GROUP_DIM, (gi + 1) * GROUP_DIM)
            mixed = _dot(ws_ref[gi], vn_ref[rows, cols]) + bs_ref[gi]
            mix_ref[rows, RET_WIDTH + gi * GROUP_DIM:RET_WIDTH + (gi + 1) * GROUP_DIM] = (
                gu_ref[rows, cols].astype(F32) * mixed).astype(BF16)
    o_ref[...] = x_ref[...] + _mod_slice(mod_ref, 5) * _dot(mix_ref[...], wo_ref[...])


def _mix_call(x1, mods3, ret, gu, vn, ws, bs, w_out, *, tiles_per_batch):
    rows = x1.shape[0]
    tm = ROW_TILE
    return pl.pallas_call(
        _mix_kernel,
        grid=(rows // tm,),
        in_specs=[
            pl.BlockSpec((tm, D_MODEL), lambda i: (i, 0)),
            pl.BlockSpec((1, 1, N_MOD * D_MODEL), lambda i: (i // tiles_per_batch, 0, 0)),
            pl.BlockSpec((HEADS, tm, HEAD_DIM), lambda i: (0, i, 0)),
            pl.BlockSpec((tm, GMLP_WIDTH), lambda i: (i, 0)),
            pl.BlockSpec((tm, GMLP_WIDTH), lambda i: (i, 0)),
            pl.BlockSpec(ws.shape, lambda i: (0, 0, 0)),
            pl.BlockSpec(bs.shape, lambda i: (0, 0, 0)),
            pl.BlockSpec(w_out.shape, lambda i: (0, 0)),
        ],
        out_specs=pl.BlockSpec((tm, D_MODEL), lambda i: (i, 0)),
        out_shape=jax.ShapeDtypeStruct((rows, D_MODEL), F32),
        scratch_shapes=[pltpu.VMEM((tm, RET_WIDTH + GMLP_WIDTH), BF16)],
        compiler_params=pltpu.CompilerParams(
            dimension_semantics=("parallel",), vmem_limit_bytes=VMEM_LIMIT),
        name="mix",
    )(x1, mods3, ret, gu, vn, ws, bs, w_out)


def _rope_tables(seq):
    rows = seq // GRID_W
    row = jnp.repeat(jnp.arange(rows, dtype=F32), GRID_W)
    col = jnp.tile(jnp.arange(GRID_W, dtype=F32), rows)
    n_freq = HEAD_DIM // 4
    freqs = ROPE_BASE ** (-jnp.arange(n_freq, dtype=F32) / n_freq)
    ang = jnp.concatenate([row[:, None] * freqs, col[:, None] * freqs], axis=-1)
    cos, sin = jnp.cos(ang), jnp.sin(ang)
    return jnp.concatenate([cos, cos], axis=-1), jnp.concatenate([-sin, sin], axis=-1)


def kernel(x, c, ctx, c_ctx, w_ada, b_ada, norm_g, ffn1_w1, ffn1_w2, w_in, ret_decay,
           gmlp_ws, gmlp_bs, w_out, ffn2_w1, ffn2_w2, final_g):
    batch, seq, d = x.shape
    ctx_len = ctx.shape[1]
    assert w_ada.shape[0] == 1, "single trunk layer only"
    assert d == D_MODEL and seq % ROW_TILE == 0 and (batch * ctx_len) % ROW_TILE == 0
    assert batch + 1 <= MOD_ROWS
    tiles_per_batch = seq // ROW_TILE

    xr = x.reshape(batch * seq, d)
    ctxr = ctx.reshape(batch * ctx_len, d)
    cc = jnp.concatenate([c, c_ctx[None], jnp.zeros((MOD_ROWS - batch - 1, d), F32)], axis=0)
    mods3 = _mods_call(cc, w_ada[0], b_ada[0][None]).reshape(MOD_ROWS, 1, N_MOD * d)

    bf = lambda w: w.astype(BF16)
    g0, g1, g2 = norm_g[0, 0][None], norm_g[0, 1][None], norm_g[0, 2][None]
    w1a, w2a = bf(ffn1_w1[0]), bf(ffn1_w2[0])
    x1 = _ffn_call(xr, mods3, g0, w1a, w2a, k0=0, tiles_per_batch=tiles_per_batch, mod_row0=0)
    c1 = _ffn_call(ctxr, mods3, g0, w1a, w2a, k0=0,
                   tiles_per_batch=(batch * ctx_len) // ROW_TILE, mod_row0=batch)

    w_in_b = bf(w_in[0])
    q, k, v, sg, gu, vn = _proj_call(
        x1, mods3, g1, w_in_b, outs=("q", "k", "v", "g", "u", "vg"),
        tiles_per_batch=tiles_per_batch, mod_row0=0, rope_tabs=_rope_tables(seq))
    kc, vc = _proj_call(c1, mods3, g1, w_in_b, outs=("k", "v"),
                        tiles_per_batch=(batch * ctx_len) // ROW_TILE, mod_row0=batch)

    rd = jnp.broadcast_to(ret_decay[0].astype(F32)[:, :, None, None], (2, HEADS, 8, HEAD_DIM))
    ret = _ret_call(rd, kc, vc, q, k, v, sg, batch=batch, seq=seq, ctx_len=ctx_len)

    bs = gmlp_bs[0][:, :, None]
    x2 = _mix_call(x1, mods3, ret, gu, vn, bf(gmlp_ws[0]), bs, bf(w_out[0]),
                   tiles_per_batch=tiles_per_batch)
    out = _ffn_call(x2, mods3, g2, bf(ffn2_w1[0]), bf(ffn2_w2[0]), k0=6,
                    tiles_per_batch=tiles_per_batch, mod_row0=0, final_g=final_g[None])
    return out.reshape(batch, seq, d)
```

```python
import functools
import math

import jax
import jax.numpy as jnp
from jax import lax
from jax.experimental import pallas as pl
from jax.experimental.pallas import tpu as pltpu

F32 = jnp.float32
BF16 = jnp.bfloat16

D_MODEL = 1024
CHUNK = 128
HEADS = 4
HEAD_DIM = 128
GROUPS = 4
GROUP_DIM = 128
RET_WIDTH = HEADS * HEAD_DIM
GMLP_WIDTH = GROUPS * GROUP_DIM
D_FF = 2816
N_MOD = 9
GRID_W = 64
ROPE_BASE = 10000.0
EPS = 1e-6

ROW_TILE = 1024
FF_TILE = 256
MOD_ROWS = 8
RET_UNROLL = 32
VMEM_LIMIT = 56 * 1024 * 1024


def _rmsnorm(x, g):
    return x * lax.rsqrt(jnp.mean(x * x, axis=-1, keepdims=True) + EPS) * g


def _mod_slice(mod_ref, k):
    return mod_ref[0, :, k * D_MODEL:(k + 1) * D_MODEL]


def _dot(a, b):
    return jnp.dot(a, b, preferred_element_type=F32)


def _mods_kernel(c_ref, w_ref, b_ref, o_ref):
    s = jax.nn.silu(c_ref[...]).astype(BF16)
    o_ref[...] = _dot(s, w_ref[...].astype(BF16)) + b_ref[...]


def _mods_call(cc, w_ada, b_ada):
    n_out = w_ada.shape[1]
    tn = D_MODEL
    return pl.pallas_call(
        _mods_kernel,
        grid=(n_out // tn,),
        in_specs=[
            pl.BlockSpec((MOD_ROWS, D_MODEL), lambda j: (0, 0)),
            pl.BlockSpec((D_MODEL, tn), lambda j: (0, j)),
            pl.BlockSpec((1, tn), lambda j: (0, j)),
        ],
        out_specs=pl.BlockSpec((MOD_ROWS, tn), lambda j: (0, j)),
        out_shape=jax.ShapeDtypeStruct((MOD_ROWS, n_out), F32),
        compiler_params=pltpu.CompilerParams(
            dimension_semantics=("arbitrary",), vmem_limit_bytes=VMEM_LIMIT),
        name="mods",
    )(cc, w_ada, b_ada)


def _ffn_kernel(x_ref, mod_ref, g_ref, w1g_ref, w1u_ref, w2_ref, *rest, k0, final_norm):
    if final_norm:
        fg_ref, o_ref, xn_ref, acc_ref = rest
    else:
        o_ref, xn_ref, acc_ref = rest
    j = pl.program_id(1)

    @pl.when(j == 0)
    def _():
        y = _rmsnorm(x_ref[...], g_ref[...])
        xn_ref[...] = (y * (1 + _mod_slice(mod_ref, k0 + 1)) + _mod_slice(mod_ref, k0)).astype(BF16)
        acc_ref[...] = jnp.zeros_like(acc_ref)

    xn = xn_ref[...]
    gate = _dot(xn, w1g_ref[...])
    up = _dot(xn, w1u_ref[...])
    h = (jax.nn.silu(gate) * up).astype(BF16)
    acc_ref[...] += _dot(h, w2_ref[...])

    @pl.when(j == pl.num_programs(1) - 1)
    def _():
        out = x_ref[...] + 0.5 * _mod_slice(mod_ref, k0 + 2) * acc_ref[...]
        if final_norm:
            out = _rmsnorm(out, fg_ref[...])
        o_ref[...] = out


def _ffn_call(xr, mods3, g, w1, w2, *, k0, tiles_per_batch, mod_row0, final_g=None):
    rows = xr.shape[0]
    tm, tf = ROW_TILE, FF_TILE
    n_ff = D_FF // tf
    mod_map = lambda i, j: (mod_row0 + i // tiles_per_batch, 0, 0)
    in_specs = [
        pl.BlockSpec((tm, D_MODEL), lambda i, j: (i, 0)),
        pl.BlockSpec((1, 1, N_MOD * D_MODEL), mod_map),
        pl.BlockSpec((1, D_MODEL), lambda i, j: (0, 0)),
        pl.BlockSpec((D_MODEL, tf), lambda i, j: (0, j)),
        pl.BlockSpec((D_MODEL, tf), lambda i, j: (0, j + n_ff)),
        pl.BlockSpec((tf, D_MODEL), lambda i, j: (j, 0)),
    ]
    args = [xr, mods3, g, w1, w1, w2]
    if final_g is not None:
        in_specs.append(pl.BlockSpec((1, D_MODEL), lambda i, j: (0, 0)))
        args.append(final_g)
    return pl.pallas_call(
        functools.partial(_ffn_kernel, k0=k0, final_norm=final_g is not None),
        grid=(rows // tm, n_ff),
        in_specs=in_specs,
        out_specs=pl.BlockSpec((tm, D_MODEL), lambda i, j: (i, 0)),
        out_shape=jax.ShapeDtypeStruct((rows, D_MODEL), F32),
        scratch_shapes=[pltpu.VMEM((tm, D_MODEL), BF16), pltpu.VMEM((tm, D_MODEL), F32)],
        compiler_params=pltpu.CompilerParams(
            dimension_semantics=("parallel", "arbitrary"), vmem_limit_bytes=VMEM_LIMIT),
        name="ffn_final" if final_g is not None else "ffn",
    )(*args)


_PROJ_COL = {"q": 0, "k": 1, "v": 2, "g": 3, "u": 4, "vg": 5}


def _proj_kernel(x_ref, mod_ref, g_ref, w_ref, *rest, outs, rope):
    if rope:
        cs_ref, sn_ref = rest[:2]
        rest = rest[2:]
    o_refs = dict(zip(outs, rest))
    y = _rmsnorm(x_ref[...], g_ref[...])
    xn = (y * (1 + _mod_slice(mod_ref, 4)) + _mod_slice(mod_ref, 3)).astype(BF16)
    k_scale = HEAD_DIM ** -0.5
    for name in outs:
        col = _PROJ_COL[name] * RET_WIDTH
        p = _dot(xn, w_ref[:, col:col + RET_WIDTH])
        o_ref = o_refs[name]
        if name in ("q", "k", "v", "g"):
            for h in range(HEADS):
                t = p[:, h * HEAD_DIM:(h + 1) * HEAD_DIM]
                if name in ("q", "k") and rope:
                    t = t * cs_ref[...] + pltpu.roll(t, HEAD_DIM // 2, 1) * sn_ref[...]
                if name == "k":
                    t = t * k_scale
                if name == "g":
                    t = jax.nn.silu(t)
                if name == "k":
                    for cc in range(t.shape[0] // CHUNK):
                        o_ref[h, cc] = t[cc * CHUNK:(cc + 1) * CHUNK, :].T.astype(BF16)
                else:
                    o_ref[h] = t.astype(BF16)
        elif name == "u":
            o_ref[...] = jax.nn.gelu(p).astype(BF16)
        else:
            gv = jax.nn.gelu(p)
            for gi in range(GROUPS):
                t = gv[:, gi * GROUP_DIM:(gi + 1) * GROUP_DIM]
                mu = jnp.mean(t, axis=-1, keepdims=True)
                d = t - mu
                var = jnp.mean(d * d, axis=-1, keepdims=True)
                o_ref[:, gi * GROUP_DIM:(gi + 1) * GROUP_DIM] = (d * lax.rsqrt(var + EPS)).astype(BF16)


def _proj_call(xr, mods3, g, w_in, *, outs, tiles_per_batch, mod_row0, rope_tabs=None):
    rows = xr.shape[0]
    tm = ROW_TILE
    mod_map = lambda i: (mod_row0 + i // tiles_per_batch, 0, 0)
    in_specs = [
        pl.BlockSpec((tm, D_MODEL), lambda i: (i, 0)),
        pl.BlockSpec((1, 1, N_MOD * D_MODEL), mod_map),
        pl.BlockSpec((1, D_MODEL), lambda i: (0, 0)),
        pl.BlockSpec(w_in.shape, lambda i: (0, 0)),
    ]
    args = [xr, mods3, g, w_in]
    if rope_tabs is not None:
        tab_map = lambda i: (i % tiles_per_batch, 0)
        in_specs += [pl.BlockSpec((tm, HEAD_DIM), tab_map)] * 2
        args += list(rope_tabs)
    out_specs, out_shape = [], []
    for name in outs:
        if name == "k":
            out_specs.append(pl.BlockSpec((HEADS, tm // CHUNK, HEAD_DIM, CHUNK), lambda i: (0, i, 0, 0)))
            out_shape.append(jax.ShapeDtypeStruct((HEADS, rows // CHUNK, HEAD_DIM, CHUNK), BF16))
        elif name in ("q", "v", "g"):
            out_specs.append(pl.BlockSpec((HEADS, tm, HEAD_DIM), lambda i: (0, i, 0)))
            out_shape.append(jax.ShapeDtypeStruct((HEADS, rows, HEAD_DIM), BF16))
        else:
            out_specs.append(pl.BlockSpec((tm, GMLP_WIDTH), lambda i: (i, 0)))
            out_shape.append(jax.ShapeDtypeStruct((rows, GMLP_WIDTH), BF16))
    return pl.pallas_call(
        functools.partial(_proj_kernel, outs=outs, rope=rope_tabs is not None),
        grid=(rows // tm,),
        in_specs=in_specs,
        out_specs=out_specs,
        out_shape=out_shape,
        compiler_params=pltpu.CompilerParams(
            dimension_semantics=("parallel",), vmem_limit_bytes=VMEM_LIMIT),
        name="proj_x" if rope_tabs is not None else "proj_ctx",
    )(*args)


def _ret_kernel(rd_ref, ktc_ref, vc_ref, q_ref, kt_ref, v_ref, sg_ref, o_ref, u_ref, s_ref,
                *, n_chunks, n_ctx_chunks):
    L = CHUNK
    lg_f = -jnp.exp(rd_ref[0, 0, 0:1, :])
    lg_b = -jnp.exp(rd_ref[1, 0, 0:1, :])
    ii = lax.broadcasted_iota(jnp.int32, (L, L), 0).astype(F32)
    jj = lax.broadcasted_iota(jnp.int32, (L, L), 1).astype(F32)
    diff = ii - jj
    lower = diff >= 0
    upper = diff <= 0
    mask = (jnp.where(lower, jnp.exp(jnp.where(lower, diff, 0.0) * lg_f), 0.0)
            + jnp.where(upper, jnp.exp(jnp.where(upper, -diff, 0.0) * lg_b), 0.0))
    xi_f = jnp.exp((ii + 1.0) * lg_f)
    ze_f = jnp.exp((L - 1.0 - ii) * lg_f)
    xi_b = jnp.exp((L - ii) * lg_b)
    ze_b = jnp.exp(ii * lg_b)
    cd_f = jnp.exp(L * lg_f)
    cd_b = jnp.exp(L * lg_b)

    def kv_pair(kt, v):
        vf = v.astype(F32)
        vz = jnp.concatenate([(vf * ze_f).astype(BF16), (vf * ze_b).astype(BF16)], axis=1)
        return _dot(kt, vz)

    def chunk_rows(c):
        return pl.ds(pl.multiple_of(c * L, L), L)

    uc = [kv_pair(ktc_ref[0, c], vc_ref[0, c * L:(c + 1) * L, :]) for c in range(n_ctx_chunks)]
    s_f = jnp.zeros((HEAD_DIM, HEAD_DIM), F32)
    for c in range(n_ctx_chunks):
        s_f = cd_f * s_f + uc[c][:, :HEAD_DIM]
    s_b = jnp.zeros((HEAD_DIM, HEAD_DIM), F32)
    for c in reversed(range(n_ctx_chunks)):
        s_b = cd_b * s_b + uc[c][:, HEAD_DIM:]

    def u_body(c, carry):
        u_ref[c] = kv_pair(kt_ref[0, c], v_ref[0, chunk_rows(c), :])
        return carry

    lax.fori_loop(0, n_chunks, u_body, 0, unroll=RET_UNROLL)

    def f_body(c, s):
        s_ref[c, 0:HEAD_DIM, :] = s.astype(BF16)
        return cd_f * s + u_ref[c, :, 0:HEAD_DIM]

    lax.fori_loop(0, n_chunks, f_body, s_f, unroll=RET_UNROLL)

    def b_body(t, s):
        c = n_chunks - 1 - t
        s_ref[c, HEAD_DIM:2 * HEAD_DIM, :] = s.astype(BF16)
        return cd_b * s + u_ref[c, :, HEAD_DIM:2 * HEAD_DIM]

    lax.fori_loop(0, n_chunks, b_body, s_b, unroll=RET_UNROLL)

    def o_body(c, carry):
        rows = chunk_rows(c)
        q = q_ref[0, rows, :]
        qf = q.astype(F32)
        p = _dot(q, kt_ref[0, c])
        lhs = jnp.concatenate(
            [(p * mask).astype(BF16), (qf * xi_f).astype(BF16), (qf * xi_b).astype(BF16)], axis=1)
        rhs = jnp.concatenate([v_ref[0, rows, :], s_ref[c]], axis=0)
        o = _dot(lhs, rhs)
        y = o * lax.rsqrt(jnp.mean(o * o, axis=-1, keepdims=True) + EPS)
        o_ref[0, rows, :] = (y * sg_ref[0, rows, :].astype(F32)).astype(BF16)
        return carry

    lax.fori_loop(0, n_chunks, o_body, 0, unroll=RET_UNROLL)


def _ret_call(rd, ktc, vc, q, kt, v, sg, *, batch, seq, ctx_len):
    n_chunks = seq // CHUNK
    n_ctx_chunks = ctx_len // CHUNK
    x_spec = pl.BlockSpec((1, seq, HEAD_DIM), lambda b, h: (h, b, 0))
    kt_spec = pl.BlockSpec((1, n_chunks, HEAD_DIM, CHUNK), lambda b, h: (h, b, 0, 0))
    vc_spec = pl.BlockSpec((1, ctx_len, HEAD_DIM), lambda b, h: (h, b, 0))
    ktc_spec = pl.BlockSpec((1, n_ctx_chunks, HEAD_DIM, CHUNK), lambda b, h: (h, b, 0, 0))
    return pl.pallas_call(
        functools.partial(_ret_kernel, n_chunks=n_chunks, n_ctx_chunks=n_ctx_chunks),
        grid=(batch, HEADS),
        in_specs=[pl.BlockSpec((2, 1, 8, HEAD_DIM), lambda b, h: (0, h, 0, 0)),
                  ktc_spec, vc_spec, x_spec, kt_spec, x_spec, x_spec],
        out_specs=x_spec,
        out_shape=jax.ShapeDtypeStruct((HEADS, batch * seq, HEAD_DIM), BF16),
        scratch_shapes=[pltpu.VMEM((n_chunks, HEAD_DIM, 2 * HEAD_DIM), F32),
                        pltpu.VMEM((n_chunks, 2 * HEAD_DIM, HEAD_DIM), BF16)],
        compiler_params=pltpu.CompilerParams(
            dimension_semantics=("parallel", "parallel"), vmem_limit_bytes=VMEM_LIMIT),
        name="retention",
    )(rd, ktc, vc, q, kt, v, sg)


def _mix_kernel(x_ref, mod_ref, ret_ref, gu_ref, vn_ref, ws_ref, bs_ref, wo_ref, o_ref, mix_ref):
    tm = x_ref.shape[0]
    for h in range(HEADS):
        mix_ref[:, h * HEAD_DIM:(h + 1) * HEAD_DIM] = ret_ref[h]
    for c in range(tm // CHUNK):
        rows = slice(c * CHUNK, (c + 1) * CHUNK)
        for gi in range(GROUPS):
            cols = slice(gi * GROUP_DIM, (gi + 1) * GROUP_DIM)
            mixed = _dot(ws_ref[gi], vn_ref[rows, cols]) + bs_ref[gi]
            mix_ref[rows, RET_WIDTH + gi * GROUP_DIM:RET_WIDTH + (gi + 1) * GROUP_DIM] = (
                gu_ref[rows, cols].astype(F32) * mixed).astype(BF16)
    o_ref[...] = x_ref[...] + _mod_slice(mod_ref, 5) * _dot(mix_ref[...], wo_ref[...])


def _mix_call(x1, mods3, ret, gu, vn, ws, bs, w_out, *, tiles_per_batch):
    rows = x1.shape[0]
    tm = ROW_TILE
    return pl.pallas_call(
        _mix_kernel,
        grid=(rows // tm,),
        in_specs=[
            pl.BlockSpec((tm, D_MODEL), lambda i: (i, 0)),
            pl.BlockSpec((1, 1, N_MOD * D_MODEL), lambda i: (i // tiles_per_batch, 0, 0)),
            pl.BlockSpec((HEADS, tm, HEAD_DIM), lambda i: (0, i, 0)),
            pl.BlockSpec((tm, GMLP_WIDTH), lambda i: (i, 0)),
            pl.BlockSpec((tm, GMLP_WIDTH), lambda i: (i, 0)),
            pl.BlockSpec(ws.shape, lambda i: (0, 0, 0)),
            pl.BlockSpec(bs.shape, lambda i: (0, 0, 0)),
            pl.BlockSpec(w_out.shape, lambda i: (0, 0)),
        ],
        out_specs=pl.BlockSpec((tm, D_MODEL), lambda i: (i, 0)),
        out_shape=jax.ShapeDtypeStruct((rows, D_MODEL), F32),
        scratch_shapes=[pltpu.VMEM((tm, RET_WIDTH + GMLP_WIDTH), BF16)],
        compiler_params=pltpu.CompilerParams(
            dimension_semantics=("parallel",), vmem_limit_bytes=VMEM_LIMIT),
        name="mix",
    )(x1, mods3, ret, gu, vn, ws, bs, w_out)


def _rope_tables(seq):
    rows = seq // GRID_W
    row = jnp.repeat(jnp.arange(rows, dtype=F32), GRID_W)
    col = jnp.tile(jnp.arange(GRID_W, dtype=F32), rows)
    n_freq = HEAD_DIM // 4
    freqs = ROPE_BASE ** (-jnp.arange(n_freq, dtype=F32) / n_freq)
    ang = jnp.concatenate([row[:, None] * freqs, col[:, None] * freqs], axis=-1)
    cos, sin = jnp.cos(ang), jnp.sin(ang)
    return jnp.concatenate([cos, cos], axis=-1), jnp.concatenate([-sin, sin], axis=-1)


def kernel(x, c, ctx, c_ctx, w_ada, b_ada, norm_g, ffn1_w1, ffn1_w2, w_in, ret_decay,
           gmlp_ws, gmlp_bs, w_out, ffn2_w1, ffn2_w2, final_g):
    batch, seq, d = x.shape
    ctx_len = ctx.shape[1]
    assert w_ada.shape[0] == 1, "single trunk layer only"
    assert d == D_MODEL and seq % ROW_TILE == 0 and (batch * ctx_len) % ROW_TILE == 0
    assert batch + 1 <= MOD_ROWS
    tiles_per_batch = seq // ROW_TILE

    xr = x.reshape(batch * seq, d)
    ctxr = ctx.reshape(batch * ctx_len, d)
    cc = jnp.concatenate([c, c_ctx[None], jnp.zeros((MOD_ROWS - batch - 1, d), F32)], axis=0)
    mods3 = _mods_call(cc, w_ada[0], b_ada[0][None]).reshape(MOD_ROWS, 1, N_MOD * d)

    bf = lambda w: w.astype(BF16)
    g0, g1, g2 = norm_g[0, 0][None], norm_g[0, 1][None], norm_g[0, 2][None]
    w1a, w2a = bf(ffn1_w1[0]), bf(ffn1_w2[0])
    x1 = _ffn_call(xr, mods3, g0, w1a, w2a, k0=0, tiles_per_batch=tiles_per_batch, mod_row0=0)
    c1 = _ffn_call(ctxr, mods3, g0, w1a, w2a, k0=0,
                   tiles_per_batch=(batch * ctx_len) // ROW_TILE, mod_row0=batch)

    w_in_b = bf(w_in[0])
    q, kt, v, sg, gu, vn = _proj_call(
        x1, mods3, g1, w_in_b, outs=("q", "k", "v", "g", "u", "vg"),
        tiles_per_batch=tiles_per_batch, mod_row0=0, rope_tabs=_rope_tables(seq))
    ktc, vc = _proj_call(c1, mods3, g1, w_in_b, outs=("k", "v"),
                        tiles_per_batch=(batch * ctx_len) // ROW_TILE, mod_row0=batch)

    rd = jnp.broadcast_to(ret_decay[0].astype(F32)[:, :, None, None], (2, HEADS, 8, HEAD_DIM))
    ret = _ret_call(rd, ktc, vc, q, kt, v, sg, batch=batch, seq=seq, ctx_len=ctx_len)

    bs = gmlp_bs[0][:, :, None]
    x2 = _mix_call(x1, mods3, ret, gu, vn, bf(gmlp_ws[0]), bs, bf(w_out[0]),
                   tiles_per_batch=tiles_per_batch)
    out = _ffn_call(x2, mods3, g2, bf(ffn2_w1[0]), bf(ffn2_w2[0]), k0=6,
                    tiles_per_batch=tiles_per_batch, mod_row0=0, final_g=final_g[None])
    return out.reshape(batch, seq, d)
```

```python
import functools
import math

import jax
import jax.numpy as jnp
from jax import lax
from jax.experimental import pallas as pl
from jax.experimental.pallas import tpu as pltpu

F32 = jnp.float32
BF16 = jnp.bfloat16

D_MODEL = 1024
CHUNK = 128
HEADS = 4
HEAD_DIM = 128
GROUPS = 4
GROUP_DIM = 128
RET_WIDTH = HEADS * HEAD_DIM
GMLP_WIDTH = GROUPS * GROUP_DIM
D_FF = 2816
N_MOD = 9
GRID_W = 64
ROPE_BASE = 10000.0
EPS = 1e-6

ROW_TILE = 1024
FFN_ROW_TILE = 512
FF_TILE = 256
BF16_ROWS = 16
MOD_ROWS = 8
RET_UNROLL = 32
VMEM_LIMIT = 56 * 1024 * 1024


def _rmsnorm(x, g):
    return x * lax.rsqrt(jnp.mean(x * x, axis=-1, keepdims=True) + EPS) * g


def _mod_slice(mod_ref, k):
    return mod_ref[0, :, k * D_MODEL:(k + 1) * D_MODEL]


def _dot(a, b):
    return jnp.dot(a, b, preferred_element_type=F32)


def _mods_kernel(c_ref, w_ref, b_ref, o_ref):
    s = jax.nn.silu(c_ref[...]).astype(BF16)
    o_ref[...] = _dot(s, w_ref[...].astype(BF16)) + b_ref[...]


def _mods_call(cc, w_ada, b_ada):
    n_out = w_ada.shape[1]
    tn = D_MODEL
    return pl.pallas_call(
        _mods_kernel,
        grid=(n_out // tn,),
        in_specs=[
            pl.BlockSpec((MOD_ROWS, D_MODEL), lambda j: (0, 0)),
            pl.BlockSpec((D_MODEL, tn), lambda j: (0, j)),
            pl.BlockSpec((1, tn), lambda j: (0, j)),
        ],
        out_specs=pl.BlockSpec((MOD_ROWS, tn), lambda j: (0, j)),
        out_shape=jax.ShapeDtypeStruct((MOD_ROWS, n_out), F32),
        compiler_params=pltpu.CompilerParams(
            dimension_semantics=("arbitrary",), vmem_limit_bytes=VMEM_LIMIT),
        name="mods",
    )(cc, w_ada, b_ada)


def _ffn_kernel(x_ref, xnext_ref, mod_ref, modn_ref, g_ref, w1_ref, w2_ref, *rest, k0, final_norm):
    if final_norm:
        fg_ref, o_ref, xn_ref = rest
    else:
        o_ref, xn_ref = rest
    i = pl.program_id(0)
    slot = i % 2
    tm = x_ref.shape[0]

    def normalized(x, m_ref):
        y = _rmsnorm(x, g_ref[...])
        return (y * (1 + _mod_slice(m_ref, k0 + 1)) + _mod_slice(m_ref, k0)).astype(BF16)

    @pl.when(i == 0)
    def _():
        xn_ref[0] = normalized(x_ref[...], mod_ref)

    n_sub = D_FF // FF_TILE
    piece = -(-tm // (n_sub * BF16_ROWS)) * BF16_ROWS
    acc = None
    for s in range(n_sub):
        cols = slice(s * FF_TILE, (s + 1) * FF_TILE)
        gate = _dot(xn_ref[slot], w1_ref[:, cols])
        up = _dot(xn_ref[slot], w1_ref[:, D_FF + s * FF_TILE:D_FF + (s + 1) * FF_TILE])
        h = (jax.nn.silu(gate) * up).astype(BF16)
        d = _dot(h, w2_ref[cols, :])
        acc = d if acc is None else acc + d
        r0 = min(s * piece, tm - piece)
        xn_ref[1 - slot, r0:r0 + piece, :] = normalized(xnext_ref[r0:r0 + piece, :], modn_ref)

    out = x_ref[...] + 0.5 * _mod_slice(mod_ref, k0 + 2) * acc
    if final_norm:
        out = _rmsnorm(out, fg_ref[...])
    o_ref[...] = out


def _ffn_call(xr, mods3, g, w1, w2, *, k0, tiles_per_batch, mod_row0, final_g=None):
    rows = xr.shape[0]
    tm = min(FFN_ROW_TILE, rows)
    n_tiles = rows // tm
    tpb = tiles_per_batch * (ROW_TILE // tm)
    nxt = lambda i: jnp.minimum(i + 1, n_tiles - 1)
    resident = dict(pipeline_mode=pl.Buffered(1))
    in_specs = [
        pl.BlockSpec((tm, D_MODEL), lambda i: (i, 0)),
        pl.BlockSpec((tm, D_MODEL), lambda i: (nxt(i), 0)),
        pl.BlockSpec((1, 1, N_MOD * D_MODEL), lambda i: (mod_row0 + i // tpb, 0, 0)),
        pl.BlockSpec((1, 1, N_MOD * D_MODEL), lambda i: (mod_row0 + nxt(i) // tpb, 0, 0)),
        pl.BlockSpec((1, D_MODEL), lambda i: (0, 0)),
        pl.BlockSpec(w1.shape, lambda i: (0, 0), **resident),
        pl.BlockSpec(w2.shape, lambda i: (0, 0), **resident),
    ]
    args = [xr, xr, mods3, mods3, g, w1, w2]
    if final_g is not None:
        in_specs.append(pl.BlockSpec((1, D_MODEL), lambda i: (0, 0)))
        args.append(final_g)
    return pl.pallas_call(
        functools.partial(_ffn_kernel, k0=k0, final_norm=final_g is not None),
        grid=(n_tiles,),
        in_specs=in_specs,
        out_specs=pl.BlockSpec((tm, D_MODEL), lambda i: (i, 0)),
        out_shape=jax.ShapeDtypeStruct((rows, D_MODEL), F32),
        scratch_shapes=[pltpu.VMEM((2, tm, D_MODEL), BF16)],
        compiler_params=pltpu.CompilerParams(
            dimension_semantics=("arbitrary",), vmem_limit_bytes=VMEM_LIMIT),
        name="ffn_final" if final_g is not None else "ffn",
    )(*args)


_PROJ_COL = {"q": 0, "k": 1, "v": 2, "g": 3, "u": 4, "vg": 5}


def _proj_kernel(x_ref, mod_ref, g_ref, w_ref, *rest, outs, rope):
    if rope:
        cs_ref, sn_ref = rest[:2]
        rest = rest[2:]
    o_refs = dict(zip(outs, rest))
    y = _rmsnorm(x_ref[...], g_ref[...])
    xn = (y * (1 + _mod_slice(mod_ref, 4)) + _mod_slice(mod_ref, 3)).astype(BF16)
    k_scale = HEAD_DIM ** -0.5
    for name in outs:
        col = _PROJ_COL[name] * RET_WIDTH
        p = _dot(xn, w_ref[:, col:col + RET_WIDTH])
        o_ref = o_refs[name]
        if name in ("q", "k", "v", "g"):
            for h in range(HEADS):
                t = p[:, h * HEAD_DIM:(h + 1) * HEAD_DIM]
                if name in ("q", "k") and rope:
                    t = t * cs_ref[...] + pltpu.roll(t, HEAD_DIM // 2, 1) * sn_ref[...]
                if name == "k":
                    t = t * k_scale
                if name == "g":
                    t = jax.nn.silu(t)
                if name == "k":
                    for cc in range(t.shape[0] // CHUNK):
                        o_ref[h, cc] = t[cc * CHUNK:(cc + 1) * CHUNK, :].T.astype(BF16)
                else:
                    o_ref[h] = t.astype(BF16)
        elif name == "u":
            o_ref[...] = jax.nn.gelu(p).astype(BF16)
        else:
            gv = jax.nn.gelu(p)
            for gi in range(GROUPS):
                t = gv[:, gi * GROUP_DIM:(gi + 1) * GROUP_DIM]
                mu = jnp.mean(t, axis=-1, keepdims=True)
                d = t - mu
                var = jnp.mean(d * d, axis=-1, keepdims=True)
                o_ref[:, gi * GROUP_DIM:(gi + 1) * GROUP_DIM] = (d * lax.rsqrt(var + EPS)).astype(BF16)


def _proj_call(xr, mods3, g, w_in, *, outs, tiles_per_batch, mod_row0, rope_tabs=None):
    rows = xr.shape[0]
    tm = ROW_TILE
    mod_map = lambda i: (mod_row0 + i // tiles_per_batch, 0, 0)
    in_specs = [
        pl.BlockSpec((tm, D_MODEL), lambda i: (i, 0)),
        pl.BlockSpec((1, 1, N_MOD * D_MODEL), mod_map),
        pl.BlockSpec((1, D_MODEL), lambda i: (0, 0)),
        pl.BlockSpec(w_in.shape, lambda i: (0, 0)),
    ]
    args = [xr, mods3, g, w_in]
    if rope_tabs is not None:
        tab_map = lambda i: (i % tiles_per_batch, 0)
        in_specs += [pl.BlockSpec((tm, HEAD_DIM), tab_map)] * 2
        args += list(rope_tabs)
    out_specs, out_shape = [], []
    for name in outs:
        if name == "k":
            out_specs.append(pl.BlockSpec((HEADS, tm // CHUNK, HEAD_DIM, CHUNK), lambda i: (0, i, 0, 0)))
            out_shape.append(jax.ShapeDtypeStruct((HEADS, rows // CHUNK, HEAD_DIM, CHUNK), BF16))
        elif name in ("q", "v", "g"):
            out_specs.append(pl.BlockSpec((HEADS, tm, HEAD_DIM), lambda i: (0, i, 0)))
            out_shape.append(jax.ShapeDtypeStruct((HEADS, rows, HEAD_DIM), BF16))
        else:
            out_specs.append(pl.BlockSpec((tm, GMLP_WIDTH), lambda i: (i, 0)))
            out_shape.append(jax.ShapeDtypeStruct((rows, GMLP_WIDTH), BF16))
    return pl.pallas_call(
        functools.partial(_proj_kernel, outs=outs, rope=rope_tabs is not None),
        grid=(rows // tm,),
        in_specs=in_specs,
        out_specs=out_specs,
        out_shape=out_shape,
        compiler_params=pltpu.CompilerParams(
            dimension_semantics=("parallel",), vmem_limit_bytes=VMEM_LIMIT),
        name="proj_x" if rope_tabs is not None else "proj_ctx",
    )(*args)


def _ret_kernel(rd_ref, ktc_ref, vc_ref, q_ref, kt_ref, v_ref, sg_ref, o_ref, u_ref, s_ref,
                *, n_chunks, n_ctx_chunks):
    L = CHUNK
    lg_f = -jnp.exp(rd_ref[0, 0, 0:1, :])
    lg_b = -jnp.exp(rd_ref[1, 0, 0:1, :])
    ii = lax.broadcasted_iota(jnp.int32, (L, L), 0).astype(F32)
    jj = lax.broadcasted_iota(jnp.int32, (L, L), 1).astype(F32)
    diff = ii - jj
    lower = diff >= 0
    upper = diff <= 0
    mask = (jnp.where(lower, jnp.exp(jnp.where(lower, diff, 0.0) * lg_f), 0.0)
            + jnp.where(upper, jnp.exp(jnp.where(upper, -diff, 0.0) * lg_b), 0.0))
    xi_f = jnp.exp((ii + 1.0) * lg_f)
    ze_f = jnp.exp((L - 1.0 - ii) * lg_f)
    xi_b = jnp.exp((L - ii) * lg_b)
    ze_b = jnp.exp(ii * lg_b)
    cd_f = jnp.exp(L * lg_f)
    cd_b = jnp.exp(L * lg_b)

    def kv_pair(kt, v):
        vf = v.astype(F32)
        vz = jnp.concatenate([(vf * ze_f).astype(BF16), (vf * ze_b).astype(BF16)], axis=1)
        return _dot(kt, vz)

    def chunk_rows(c):
        return pl.ds(pl.multiple_of(c * L, L), L)

    uc = [kv_pair(ktc_ref[0, c], vc_ref[0, c * L:(c + 1) * L, :]) for c in range(n_ctx_chunks)]
    s_f = jnp.zeros((HEAD_DIM, HEAD_DIM), F32)
    for c in range(n_ctx_chunks):
        s_f = cd_f * s_f + uc[c][:, :HEAD_DIM]
    s_b = jnp.zeros((HEAD_DIM, HEAD_DIM), F32)
    for c in reversed(range(n_ctx_chunks)):
        s_b = cd_b * s_b + uc[c][:, HEAD_DIM:]

    def u_body(c, carry):
        u_ref[c] = kv_pair(kt_ref[0, c], v_ref[0, chunk_rows(c), :])
        return carry

    lax.fori_loop(0, n_chunks, u_body, 0, unroll=RET_UNROLL)

    def f_body(c, s):
        s_ref[c, 0:HEAD_DIM, :] = s.astype(BF16)
        return cd_f * s + u_ref[c, :, 0:HEAD_DIM]

    lax.fori_loop(0, n_chunks, f_body, s_f, unroll=RET_UNROLL)

    def b_body(t, s):
        c = n_chunks - 1 - t
        s_ref[c, HEAD_DIM:2 * HEAD_DIM, :] = s.astype(BF16)
        return cd_b * s + u_ref[c, :, HEAD_DIM:2 * HEAD_DIM]

    lax.fori_loop(0, n_chunks, b_body, s_b, unroll=RET_UNROLL)

    def o_body(c, carry):
        rows = chunk_rows(c)
        q = q_ref[0, rows, :]
        qf = q.astype(F32)
        p = _dot(q, kt_ref[0, c])
        lhs = jnp.concatenate(
            [(p * mask).astype(BF16), (qf * xi_f).astype(BF16), (qf * xi_b).astype(BF16)], axis=1)
        rhs = jnp.concatenate([v_ref[0, rows, :], s_ref[c]], axis=0)
        o = _dot(lhs, rhs)
        y = o * lax.rsqrt(jnp.mean(o * o, axis=-1, keepdims=True) + EPS)
        o_ref[0, rows, :] = (y * sg_ref[0, rows, :].astype(F32)).astype(BF16)
        return carry

    lax.fori_loop(0, n_chunks, o_body, 0, unroll=RET_UNROLL)


def _ret_call(rd, ktc, vc, q, kt, v, sg, *, batch, seq, ctx_len):
    n_chunks = seq // CHUNK
    n_ctx_chunks = ctx_len // CHUNK
    x_spec = pl.BlockSpec((1, seq, HEAD_DIM), lambda b, h: (h, b, 0))
    kt_spec = pl.BlockSpec((1, n_chunks, HEAD_DIM, CHUNK), lambda b, h: (h, b, 0, 0))
    vc_spec = pl.BlockSpec((1, ctx_len, HEAD_DIM), lambda b, h: (h, b, 0))
    ktc_spec = pl.BlockSpec((1, n_ctx_chunks, HEAD_DIM, CHUNK), lambda b, h: (h, b, 0, 0))
    return pl.pallas_call(
        functools.partial(_ret_kernel, n_chunks=n_chunks, n_ctx_chunks=n_ctx_chunks),
        grid=(batch, HEADS),
        in_specs=[pl.BlockSpec((2, 1, 8, HEAD_DIM), lambda b, h: (0, h, 0, 0)),
                  ktc_spec, vc_spec, x_spec, kt_spec, x_spec, x_spec],
        out_specs=x_spec,
        out_shape=jax.ShapeDtypeStruct((HEADS, batch * seq, HEAD_DIM), BF16),
        scratch_shapes=[pltpu.VMEM((n_chunks, HEAD_DIM, 2 * HEAD_DIM), F32),
                        pltpu.VMEM((n_chunks, 2 * HEAD_DIM, HEAD_DIM), BF16)],
        compiler_params=pltpu.CompilerParams(
            dimension_semantics=("parallel", "parallel"), vmem_limit_bytes=VMEM_LIMIT),
        name="retention",
    )(rd, ktc, vc, q, kt, v, sg)


def _mix_kernel(x_ref, mod_ref, ret_ref, gu_ref, vn_ref, ws_ref, bs_ref, wo_ref, o_ref, mix_ref):
    tm = x_ref.shape[0]
    for h in range(HEADS):
        mix_ref[:, h * HEAD_DIM:(h + 1) * HEAD_DIM] = ret_ref[h]
    for c in range(tm // CHUNK):
        rows = slice(c * CHUNK, (c + 1) * CHUNK)
        for gi in range(GROUPS):
            cols = slice(gi * GROUP_DIM, (gi + 1) * GROUP_DIM)
            mixed = _dot(ws_ref[gi], vn_ref[rows, cols]) + bs_ref[gi]
            mix_ref[rows, RET_WIDTH + gi * GROUP_DIM:RET_WIDTH + (gi + 1) * GROUP_DIM] = (
                gu_ref[rows, cols].astype(F32) * mixed).astype(BF16)
    o_ref[...] = x_ref[...] + _mod_slice(mod_ref, 5) * _dot(mix_ref[...], wo_ref[...])


def _mix_call(x1, mods3, ret, gu, vn, ws, bs, w_out, *, tiles_per_batch):
    rows = x1.shape[0]
    tm = ROW_TILE
    return pl.pallas_call(
        _mix_kernel,
        grid=(rows // tm,),
        in_specs=[
            pl.BlockSpec((tm, D_MODEL), lambda i: (i, 0)),
            pl.BlockSpec((1, 1, N_MOD * D_MODEL), lambda i: (i // tiles_per_batch, 0, 0)),
            pl.BlockSpec((HEADS, tm, HEAD_DIM), lambda i: (0, i, 0)),
            pl.BlockSpec((tm, GMLP_WIDTH), lambda i: (i, 0)),
            pl.BlockSpec((tm, GMLP_WIDTH), lambda i: (i, 0)),
            pl.BlockSpec(ws.shape, lambda i: (0, 0, 0)),
            pl.BlockSpec(bs.shape, lambda i: (0, 0, 0)),
            pl.BlockSpec(w_out.shape, lambda i: (0, 0)),
        ],
        out_specs=pl.BlockSpec((tm, D_MODEL), lambda i: (i, 0)),
        out_shape=jax.ShapeDtypeStruct((rows, D_MODEL), F32),
        scratch_shapes=[pltpu.VMEM((tm, RET_WIDTH + GMLP_WIDTH), BF16)],
        compiler_params=pltpu.CompilerParams(
            dimension_semantics=("parallel",), vmem_limit_bytes=VMEM_LIMIT),
        name="mix",
    )(x1, mods3, ret, gu, vn, ws, bs, w_out)


def _rope_tables(seq):
    rows = seq // GRID_W
    row = jnp.repeat(jnp.arange(rows, dtype=F32), GRID_W)
    col = jnp.tile(jnp.arange(GRID_W, dtype=F32), rows)
    n_freq = HEAD_DIM // 4
    freqs = ROPE_BASE ** (-jnp.arange(n_freq, dtype=F32) / n_freq)
    ang = jnp.concatenate([row[:, None] * freqs, col[:, None] * freqs], axis=-1)
    cos, sin = jnp.cos(ang), jnp.sin(ang)
    return jnp.concatenate([cos, cos], axis=-1), jnp.concatenate([-sin, sin], axis=-1)


def kernel(x, c, ctx, c_ctx, w_ada, b_ada, norm_g, ffn1_w1, ffn1_w2, w_in, ret_decay,
           gmlp_ws, gmlp_bs, w_out, ffn2_w1, ffn2_w2, final_g):
    batch, seq, d = x.shape
    ctx_len = ctx.shape[1]
    assert w_ada.shape[0] == 1, "single trunk layer only"
    assert d == D_MODEL and seq % ROW_TILE == 0 and (batch * ctx_len) % ROW_TILE == 0
    assert batch + 1 <= MOD_ROWS
    tiles_per_batch = seq // ROW_TILE

    xr = x.reshape(batch * seq, d)
    ctxr = ctx.reshape(batch * ctx_len, d)
    cc = jnp.concatenate([c, c_ctx[None], jnp.zeros((MOD_ROWS - batch - 1, d), F32)], axis=0)
    mods3 = _mods_call(cc, w_ada[0], b_ada[0][None]).reshape(MOD_ROWS, 1, N_MOD * d)

    bf = lambda w: w.astype(BF16)
    g0, g1, g2 = norm_g[0, 0][None], norm_g[0, 1][None], norm_g[0, 2][None]
    w1a, w2a = bf(ffn1_w1[0]), bf(ffn1_w2[0])
    x1 = _ffn_call(xr, mods3, g0, w1a, w2a, k0=0, tiles_per_batch=tiles_per_batch, mod_row0=0)
    c1 = _ffn_call(ctxr, mods3, g0, w1a, w2a, k0=0,
                   tiles_per_batch=(batch * ctx_len) // ROW_TILE, mod_row0=batch)

    w_in_b = bf(w_in[0])
    q, kt, v, sg, gu, vn = _proj_call(
        x1, mods3, g1, w_in_b, outs=("q", "k", "v", "g", "u", "vg"),
        tiles_per_batch=tiles_per_batch, mod_row0=0, rope_tabs=_rope_tables(seq))
    ktc, vc = _proj_call(c1, mods3, g1, w_in_b, outs=("k", "v"),
                        tiles_per_batch=(batch * ctx_len) // ROW_TILE, mod_row0=batch)

    rd = jnp.broadcast_to(ret_decay[0].astype(F32)[:, :, None, None], (2, HEADS, 8, HEAD_DIM))
    ret = _ret_call(rd, ktc, vc, q, kt, v, sg, batch=batch, seq=seq, ctx_len=ctx_len)

    bs = gmlp_bs[0][:, :, None]
    x2 = _mix_call(x1, mods3, ret, gu, vn, bf(gmlp_ws[0]), bs, bf(w_out[0]),
                   tiles_per_batch=tiles_per_batch)
    out = _ffn_call(x2, mods3, g2, bf(ffn2_w1[0]), bf(ffn2_w2[0]), k0=6,
                    tiles_per_batch=tiles_per_batch, mod_row0=0, final_g=final_g[None])
    return out.reshape(batch, seq, d)
```

```python
import functools
import math

import jax
import jax.numpy as jnp
from jax import lax
from jax.experimental import pallas as pl
from jax.experimental.pallas import tpu as pltpu

F32 = jnp.float32
BF16 = jnp.bfloat16

D_MODEL = 1024
CHUNK = 128
HEADS = 4
HEAD_DIM = 128
GROUPS = 4
GROUP_DIM = 128
RET_WIDTH = HEADS * HEAD_DIM
GMLP_WIDTH = GROUPS * GROUP_DIM
D_FF = 2816
N_MOD = 9
GRID_W = 64
ROPE_BASE = 10000.0
EPS = 1e-6

ROW_TILE = 1024
FFN_ROW_TILE = 512
FF_TILE = 256
BF16_ROWS = 16
MOD_ROWS = 8
RET_UNROLL = 32
VMEM_LIMIT = 56 * 1024 * 1024


def _rmsnorm(x, g):
    return x * lax.rsqrt(jnp.mean(x * x, axis=-1, keepdims=True) + EPS) * g


def _mod_slice(mod_ref, k):
    return mod_ref[0, :, k * D_MODEL:(k + 1) * D_MODEL]


def _dot(a, b):
    return jnp.dot(a, b, preferred_element_type=F32)


def _mods_kernel(c_ref, w_ref, b_ref, o_ref):
    s = jax.nn.silu(c_ref[...]).astype(BF16)
    o_ref[...] = _dot(s, w_ref[...].astype(BF16)) + b_ref[...]


def _mods_call(cc, w_ada, b_ada):
    n_out = w_ada.shape[1]
    tn = D_MODEL
    return pl.pallas_call(
        _mods_kernel,
        grid=(n_out // tn,),
        in_specs=[
            pl.BlockSpec((MOD_ROWS, D_MODEL), lambda j: (0, 0)),
            pl.BlockSpec((D_MODEL, tn), lambda j: (0, j)),
            pl.BlockSpec((1, tn), lambda j: (0, j)),
        ],
        out_specs=pl.BlockSpec((MOD_ROWS, tn), lambda j: (0, j)),
        out_shape=jax.ShapeDtypeStruct((MOD_ROWS, n_out), F32),
        compiler_params=pltpu.CompilerParams(
            dimension_semantics=("arbitrary",), vmem_limit_bytes=VMEM_LIMIT),
        name="mods",
    )(cc, w_ada, b_ada)


def _ffn_kernel(x_ref, xnext_ref, mod_ref, modn_ref, g_ref, w1_ref, w2_ref, *rest,
                k0, final_norm, n_cast):
    rest = list(rest)
    fg_ref = rest.pop(0) if final_norm else None
    cast_in = [rest.pop(0) for _ in range(n_cast)]
    o_ref = rest.pop(0)
    cast_out = [rest.pop(0) for _ in range(n_cast)]
    (xn_ref,) = rest
    i = pl.program_id(0)
    slot = i % 2
    tm = x_ref.shape[0]

    def normalized(x, m_ref):
        y = _rmsnorm(x, g_ref[...])
        return (y * (1 + _mod_slice(m_ref, k0 + 1)) + _mod_slice(m_ref, k0)).astype(BF16)

    @pl.when(i == 0)
    def _():
        xn_ref[0] = normalized(x_ref[...], mod_ref)

    n_sub = D_FF // FF_TILE
    piece = -(-tm // (n_sub * BF16_ROWS)) * BF16_ROWS
    acc = None
    for s in range(n_sub):
        cols = slice(s * FF_TILE, (s + 1) * FF_TILE)
        gate = _dot(xn_ref[slot], w1_ref[:, cols])
        up = _dot(xn_ref[slot], w1_ref[:, D_FF + s * FF_TILE:D_FF + (s + 1) * FF_TILE])
        h = (jax.nn.silu(gate) * up).astype(BF16)
        d = _dot(h, w2_ref[cols, :])
        acc = d if acc is None else acc + d
        r0 = min(s * piece, tm - piece)
        xn_ref[1 - slot, r0:r0 + piece, :] = normalized(xnext_ref[r0:r0 + piece, :], modn_ref)
        if s < len(cast_in):
            cast_out[s][...] = cast_in[s][...].astype(BF16)

    out = x_ref[...] + 0.5 * _mod_slice(mod_ref, k0 + 2) * acc
    if final_norm:
        out = _rmsnorm(out, fg_ref[...])
    o_ref[...] = out


def _ffn_call(xr, mods3, g, w1, w2, *, k0, tiles_per_batch, mod_row0, final_g=None, cast_weights=()):
    rows = xr.shape[0]
    tm = min(FFN_ROW_TILE, rows)
    n_tiles = rows // tm
    n_slabs = n_tiles // 2
    cast_specs, cast_shapes = [], []
    for w in cast_weights:
        slab = w.shape[0] // n_slabs
        assert slab * n_slabs == w.shape[0] and slab % BF16_ROWS == 0, (w.shape, n_slabs)
        cast_specs.append(pl.BlockSpec((slab, w.shape[1]), lambda i: (i // 2, 0)))
        cast_shapes.append(jax.ShapeDtypeStruct(w.shape, BF16))
    tpb = tiles_per_batch * (ROW_TILE // tm)
    nxt = lambda i: jnp.minimum(i + 1, n_tiles - 1)
    resident = dict(pipeline_mode=pl.Buffered(1))
    in_specs = [
        pl.BlockSpec((tm, D_MODEL), lambda i: (i, 0)),
        pl.BlockSpec((tm, D_MODEL), lambda i: (nxt(i), 0)),
        pl.BlockSpec((1, 1, N_MOD * D_MODEL), lambda i: (mod_row0 + i // tpb, 0, 0)),
        pl.BlockSpec((1, 1, N_MOD * D_MODEL), lambda i: (mod_row0 + nxt(i) // tpb, 0, 0)),
        pl.BlockSpec((1, D_MODEL), lambda i: (0, 0)),
        pl.BlockSpec(w1.shape, lambda i: (0, 0), **resident),
        pl.BlockSpec(w2.shape, lambda i: (0, 0), **resident),
    ]
    args = [xr, xr, mods3, mods3, g, w1, w2]
    if final_g is not None:
        in_specs.append(pl.BlockSpec((1, D_MODEL), lambda i: (0, 0)))
        args.append(final_g)
    outs = pl.pallas_call(
        functools.partial(_ffn_kernel, k0=k0, final_norm=final_g is not None, n_cast=len(cast_weights)),
        grid=(n_tiles,),
        in_specs=in_specs + cast_specs,
        out_specs=[pl.BlockSpec((tm, D_MODEL), lambda i: (i, 0))] + cast_specs,
        out_shape=[jax.ShapeDtypeStruct((rows, D_MODEL), F32)] + cast_shapes,
        scratch_shapes=[pltpu.VMEM((2, tm, D_MODEL), BF16)],
        compiler_params=pltpu.CompilerParams(
            dimension_semantics=("arbitrary",), vmem_limit_bytes=VMEM_LIMIT),
        name="ffn_final" if final_g is not None else "ffn",
    )(*args, *cast_weights)
    return outs[0] if not cast_weights else outs


_PROJ_COL = {"q": 0, "k": 1, "v": 2, "g": 3, "u": 4, "vg": 5}
_PROJ_ORDER = ("vg", "k", "q", "u", "g", "v")


def _proj_kernel(x_ref, mod_ref, g_ref, w_ref, *rest, outs, rope):
    if rope:
        cs_ref, sn_ref = rest[:2]
        rest = rest[2:]
    o_refs = dict(zip(outs, rest))
    y = _rmsnorm(x_ref[...], g_ref[...])
    xn = (y * (1 + _mod_slice(mod_ref, 4)) + _mod_slice(mod_ref, 3)).astype(BF16)
    k_scale = HEAD_DIM ** -0.5
    for name in sorted(outs, key=_PROJ_ORDER.index):
        col = _PROJ_COL[name] * RET_WIDTH
        p = _dot(xn, w_ref[:, col:col + RET_WIDTH])
        o_ref = o_refs[name]
        if name in ("q", "k", "v", "g"):
            for h in range(HEADS):
                t = p[:, h * HEAD_DIM:(h + 1) * HEAD_DIM]
                if name in ("q", "k") and rope:
                    t = t * cs_ref[...] + pltpu.roll(t, HEAD_DIM // 2, 1) * sn_ref[...]
                if name == "k":
                    t = t * k_scale
                if name == "g":
                    t = jax.nn.silu(t)
                if name == "k":
                    for cc in range(t.shape[0] // CHUNK):
                        o_ref[h, cc] = t[cc * CHUNK:(cc + 1) * CHUNK, :].T.astype(BF16)
                else:
                    o_ref[h] = t.astype(BF16)
        elif name == "u":
            o_ref[...] = jax.nn.gelu(p).astype(BF16)
        else:
            gv = jax.nn.gelu(p)
            for gi in range(GROUPS):
                t = gv[:, gi * GROUP_DIM:(gi + 1) * GROUP_DIM]
                mu = jnp.mean(t, axis=-1, keepdims=True)
                d = t - mu
                var = jnp.mean(d * d, axis=-1, keepdims=True)
                o_ref[:, gi * GROUP_DIM:(gi + 1) * GROUP_DIM] = (d * lax.rsqrt(var + EPS)).astype(BF16)


def _proj_call(xr, mods3, g, w_in, *, outs, tiles_per_batch, mod_row0, rope_tabs=None):
    rows = xr.shape[0]
    tm = ROW_TILE
    mod_map = lambda i: (mod_row0 + i // tiles_per_batch, 0, 0)
    in_specs = [
        pl.BlockSpec((tm, D_MODEL), lambda i: (i, 0)),
        pl.BlockSpec((1, 1, N_MOD * D_MODEL), mod_map),
        pl.BlockSpec((1, D_MODEL), lambda i: (0, 0)),
        pl.BlockSpec(w_in.shape, lambda i: (0, 0)),
    ]
    args = [xr, mods3, g, w_in]
    if rope_tabs is not None:
        tab_map = lambda i: (i % tiles_per_batch, 0)
        in_specs += [pl.BlockSpec((tm, HEAD_DIM), tab_map)] * 2
        args += list(rope_tabs)
    out_specs, out_shape = [], []
    for name in outs:
        if name == "k":
            out_specs.append(pl.BlockSpec((HEADS, tm // CHUNK, HEAD_DIM, CHUNK), lambda i: (0, i, 0, 0)))
            out_shape.append(jax.ShapeDtypeStruct((HEADS, rows // CHUNK, HEAD_DIM, CHUNK), BF16))
        elif name in ("q", "v", "g"):
            out_specs.append(pl.BlockSpec((HEADS, tm, HEAD_DIM), lambda i: (0, i, 0)))
            out_shape.append(jax.ShapeDtypeStruct((HEADS, rows, HEAD_DIM), BF16))
        else:
            out_specs.append(pl.BlockSpec((tm, GMLP_WIDTH), lambda i: (i, 0)))
            out_shape.append(jax.ShapeDtypeStruct((rows, GMLP_WIDTH), BF16))
    return pl.pallas_call(
        functools.partial(_proj_kernel, outs=outs, rope=rope_tabs is not None),
        grid=(rows // tm,),
        in_specs=in_specs,
        out_specs=out_specs,
        out_shape=out_shape,
        compiler_params=pltpu.CompilerParams(
            dimension_semantics=("parallel",), vmem_limit_bytes=VMEM_LIMIT),
        name="proj_x" if rope_tabs is not None else "proj_ctx",
    )(*args)


def _ret_kernel(rd_ref, ktc_ref, vc_ref, q_ref, kt_ref, v_ref, sg_ref, o_ref, u_ref, s_ref,
                *, n_chunks, n_ctx_chunks):
    L = CHUNK
    lg_f = -jnp.exp(rd_ref[0, 0, 0:1, :])
    lg_b = -jnp.exp(rd_ref[1, 0, 0:1, :])
    ii = lax.broadcasted_iota(jnp.int32, (L, L), 0).astype(F32)
    jj = lax.broadcasted_iota(jnp.int32, (L, L), 1).astype(F32)
    diff = ii - jj
    lower = diff >= 0
    upper = diff <= 0
    mask = (jnp.where(lower, jnp.exp(jnp.where(lower, diff, 0.0) * lg_f), 0.0)
            + jnp.where(upper, jnp.exp(jnp.where(upper, -diff, 0.0) * lg_b), 0.0))
    xi_f = jnp.exp((ii + 1.0) * lg_f)
    ze_f = jnp.exp((L - 1.0 - ii) * lg_f)
    xi_b = jnp.exp((L - ii) * lg_b)
    ze_b = jnp.exp(ii * lg_b)
    cd_f = jnp.exp(L * lg_f)
    cd_b = jnp.exp(L * lg_b)

    def kv_pair(kt, v):
        vf = v.astype(F32)
        vz = jnp.concatenate([(vf * ze_f).astype(BF16), (vf * ze_b).astype(BF16)], axis=1)
        return _dot(kt, vz)

    def chunk_rows(c):
        return pl.ds(pl.multiple_of(c * L, L), L)

    uc = [kv_pair(ktc_ref[0, c], vc_ref[0, c * L:(c + 1) * L, :]) for c in range(n_ctx_chunks)]
    s_f = jnp.zeros((HEAD_DIM, HEAD_DIM), F32)
    for c in range(n_ctx_chunks):
        s_f = cd_f * s_f + uc[c][:, :HEAD_DIM]
    s_b = jnp.zeros((HEAD_DIM, HEAD_DIM), F32)
    for c in reversed(range(n_ctx_chunks)):
        s_b = cd_b * s_b + uc[c][:, HEAD_DIM:]

    def u_body(c, carry):
        u_ref[c] = kv_pair(kt_ref[0, c], v_ref[0, chunk_rows(c), :])
        return carry

    lax.fori_loop(0, n_chunks, u_body, 0, unroll=RET_UNROLL)

    def f_body(c, s):
        s_ref[c, 0:HEAD_DIM, :] = s.astype(BF16)
        return cd_f * s + u_ref[c, :, 0:HEAD_DIM]

    lax.fori_loop(0, n_chunks, f_body, s_f, unroll=RET_UNROLL)

    def b_body(t, s):
        c = n_chunks - 1 - t
        s_ref[c, HEAD_DIM:2 * HEAD_DIM, :] = s.astype(BF16)
        return cd_b * s + u_ref[c, :, HEAD_DIM:2 * HEAD_DIM]

    lax.fori_loop(0, n_chunks, b_body, s_b, unroll=RET_UNROLL)

    def o_body(c, carry):
        rows = chunk_rows(c)
        q = q_ref[0, rows, :]
        qf = q.astype(F32)
        p = _dot(q, kt_ref[0, c])
        lhs = jnp.concatenate(
            [(p * mask).astype(BF16), (qf * xi_f).astype(BF16), (qf * xi_b).astype(BF16)], axis=1)
        rhs = jnp.concatenate([v_ref[0, rows, :], s_ref[c]], axis=0)
        o = _dot(lhs, rhs)
        y = o * lax.rsqrt(jnp.mean(o * o, axis=-1, keepdims=True) + EPS)
        o_ref[0, rows, :] = (y * sg_ref[0, rows, :].astype(F32)).astype(BF16)
        return carry

    lax.fori_loop(0, n_chunks, o_body, 0, unroll=RET_UNROLL)


def _ret_call(rd, ktc, vc, q, kt, v, sg, *, batch, seq, ctx_len):
    n_chunks = seq // CHUNK
    n_ctx_chunks = ctx_len // CHUNK
    x_spec = pl.BlockSpec((1, seq, HEAD_DIM), lambda b, h: (h, b, 0))
    kt_spec = pl.BlockSpec((1, n_chunks, HEAD_DIM, CHUNK), lambda b, h: (h, b, 0, 0))
    vc_spec = pl.BlockSpec((1, ctx_len, HEAD_DIM), lambda b, h: (h, b, 0))
    ktc_spec = pl.BlockSpec((1, n_ctx_chunks, HEAD_DIM, CHUNK), lambda b, h: (h, b, 0, 0))
    return pl.pallas_call(
        functools.partial(_ret_kernel, n_chunks=n_chunks, n_ctx_chunks=n_ctx_chunks),
        grid=(batch, HEADS),
        in_specs=[pl.BlockSpec((2, 1, 8, HEAD_DIM), lambda b, h: (0, h, 0, 0)),
                  ktc_spec, vc_spec, x_spec, kt_spec, x_spec, x_spec],
        out_specs=x_spec,
        out_shape=jax.ShapeDtypeStruct((HEADS, batch * seq, HEAD_DIM), BF16),
        scratch_shapes=[pltpu.VMEM((n_chunks, HEAD_DIM, 2 * HEAD_DIM), F32),
                        pltpu.VMEM((n_chunks, 2 * HEAD_DIM, HEAD_DIM), BF16)],
        compiler_params=pltpu.CompilerParams(
            dimension_semantics=("parallel", "parallel"), vmem_limit_bytes=VMEM_LIMIT),
        name="retention",
    )(rd, ktc, vc, q, kt, v, sg)


def _mix_kernel(x_ref, mod_ref, ret_ref, gu_ref, vn_ref, ws_ref, bs_ref, wo_ref, o_ref, mix_ref):
    tm = x_ref.shape[0]
    for h in range(HEADS):
        mix_ref[:, h * HEAD_DIM:(h + 1) * HEAD_DIM] = ret_ref[h]
    for c in range(tm // CHUNK):
        rows = slice(c * CHUNK, (c + 1) * CHUNK)
        for gi in range(GROUPS):
            cols = slice(gi * GROUP_DIM, (gi + 1) * GROUP_DIM)
            mixed = _dot(ws_ref[gi], vn_ref[rows, cols]) + bs_ref[gi]
            mix_ref[rows, RET_WIDTH + gi * GROUP_DIM:RET_WIDTH + (gi + 1) * GROUP_DIM] = (
                gu_ref[rows, cols].astype(F32) * mixed).astype(BF16)
    o_ref[...] = x_ref[...] + _mod_slice(mod_ref, 5) * _dot(mix_ref[...], wo_ref[...])


def _mix_call(x1, mods3, ret, gu, vn, ws, bs, w_out, *, tiles_per_batch):
    rows = x1.shape[0]
    tm = ROW_TILE
    return pl.pallas_call(
        _mix_kernel,
        grid=(rows // tm,),
        in_specs=[
            pl.BlockSpec((tm, D_MODEL), lambda i: (i, 0)),
            pl.BlockSpec((1, 1, N_MOD * D_MODEL), lambda i: (i // tiles_per_batch, 0, 0)),
            pl.BlockSpec((HEADS, tm, HEAD_DIM), lambda i: (0, i, 0)),
            pl.BlockSpec((tm, GMLP_WIDTH), lambda i: (i, 0)),
            pl.BlockSpec((tm, GMLP_WIDTH), lambda i: (i, 0)),
            pl.BlockSpec(ws.shape, lambda i: (0, 0, 0)),
            pl.BlockSpec(bs.shape, lambda i: (0, 0, 0)),
            pl.BlockSpec(w_out.shape, lambda i: (0, 0)),
        ],
        out_specs=pl.BlockSpec((tm, D_MODEL), lambda i: (i, 0)),
        out_shape=jax.ShapeDtypeStruct((rows, D_MODEL), F32),
        scratch_shapes=[pltpu.VMEM((tm, RET_WIDTH + GMLP_WIDTH), BF16)],
        compiler_params=pltpu.CompilerParams(
            dimension_semantics=("parallel",), vmem_limit_bytes=VMEM_LIMIT),
        name="mix",
    )(x1, mods3, ret, gu, vn, ws, bs, w_out)


def _rope_tables(seq):
    rows = seq // GRID_W
    row = jnp.repeat(jnp.arange(rows, dtype=F32), GRID_W)
    col = jnp.tile(jnp.arange(GRID_W, dtype=F32), rows)
    n_freq = HEAD_DIM // 4
    freqs = ROPE_BASE ** (-jnp.arange(n_freq, dtype=F32) / n_freq)
    ang = jnp.concatenate([row[:, None] * freqs, col[:, None] * freqs], axis=-1)
    cos, sin = jnp.cos(ang), jnp.sin(ang)
    return jnp.concatenate([cos, cos], axis=-1), jnp.concatenate([-sin, sin], axis=-1)


def kernel(x, c, ctx, c_ctx, w_ada, b_ada, norm_g, ffn1_w1, ffn1_w2, w_in, ret_decay,
           gmlp_ws, gmlp_bs, w_out, ffn2_w1, ffn2_w2, final_g):
    batch, seq, d = x.shape
    ctx_len = ctx.shape[1]
    assert w_ada.shape[0] == 1, "single trunk layer only"
    assert d == D_MODEL and seq % ROW_TILE == 0 and (batch * ctx_len) % ROW_TILE == 0
    assert batch + 1 <= MOD_ROWS
    tiles_per_batch = seq // ROW_TILE

    xr = x.reshape(batch * seq, d)
    ctxr = ctx.reshape(batch * ctx_len, d)
    cc = jnp.concatenate([c, c_ctx[None], jnp.zeros((MOD_ROWS - batch - 1, d), F32)], axis=0)
    mods3 = _mods_call(cc, w_ada[0], b_ada[0][None]).reshape(MOD_ROWS, 1, N_MOD * d)

    bf = lambda w: w.astype(BF16)
    g0, g1, g2 = norm_g[0, 0][None], norm_g[0, 1][None], norm_g[0, 2][None]
    w1a, w2a = bf(ffn1_w1[0]), bf(ffn1_w2[0])
    x1, w_in_b, w_out_b, w1b, w2b = _ffn_call(
        xr, mods3, g0, w1a, w2a, k0=0, tiles_per_batch=tiles_per_batch, mod_row0=0,
        cast_weights=(w_in[0], w_out[0], ffn2_w1[0], ffn2_w2[0]))
    c1 = _ffn_call(ctxr, mods3, g0, w1a, w2a, k0=0,
                   tiles_per_batch=(batch * ctx_len) // ROW_TILE, mod_row0=batch)

    q, kt, v, sg, gu, vn = _proj_call(
        x1, mods3, g1, w_in_b, outs=("q", "k", "v", "g", "u", "vg"),
        tiles_per_batch=tiles_per_batch, mod_row0=0, rope_tabs=_rope_tables(seq))
    ktc, vc = _proj_call(c1, mods3, g1, w_in_b, outs=("k", "v"),
                        tiles_per_batch=(batch * ctx_len) // ROW_TILE, mod_row0=batch)

    rd = jnp.broadcast_to(ret_decay[0].astype(F32)[:, :, None, None], (2, HEADS, 8, HEAD_DIM))
    ret = _ret_call(rd, ktc, vc, q, kt, v, sg, batch=batch, seq=seq, ctx_len=ctx_len)

    bs = gmlp_bs[0][:, :, None]
    x2 = _mix_call(x1, mods3, ret, gu, vn, bf(gmlp_ws[0]), bs, w_out_b,
                   tiles_per_batch=tiles_per_batch)
    out = _ffn_call(x2, mods3, g2, w1b, w2b, k0=6,
                    tiles_per_batch=tiles_per_batch, mod_row0=0, final_g=final_g[None])
    return out.reshape(batch, seq, d)
```

```python
import functools
import math

import jax
import jax.numpy as jnp
import numpy as np
from jax import lax
from jax.experimental import pallas as pl
from jax.experimental.pallas import tpu as pltpu

F32 = jnp.float32
BF16 = jnp.bfloat16

D_MODEL = 1024
CHUNK = 128
HEADS = 4
HEAD_DIM = 128
GROUPS = 4
GROUP_DIM = 128
RET_WIDTH = HEADS * HEAD_DIM
GMLP_WIDTH = GROUPS * GROUP_DIM
D_FF = 2816
N_MOD = 9
GRID_W = 64
ROPE_BASE = 10000.0
EPS = 1e-6

ROW_TILE = 1024
FFN_ROW_TILE = 512
FF_TILE = 256
BF16_ROWS = 16
MOD_ROWS = 8
GMLP_UNROLL = 8
RET_UNROLL = 32
VMEM_LIMIT = 56 * 1024 * 1024


def _rmsnorm(x, g):
    return x * lax.rsqrt(jnp.mean(x * x, axis=-1, keepdims=True) + EPS) * g


def _mod_slice(mod_ref, k):
    return mod_ref[0, :, k * D_MODEL:(k + 1) * D_MODEL]


def _dot(a, b):
    return jnp.dot(a, b, preferred_element_type=F32)


def _mods_kernel(c_ref, w_ref, b_ref, o_ref):
    s = jax.nn.silu(c_ref[...]).astype(BF16)
    o_ref[...] = _dot(s, w_ref[...].astype(BF16)) + b_ref[...]


def _mods_call(cc, w_ada, b_ada):
    n_out = w_ada.shape[1]
    tn = D_MODEL // 2
    return pl.pallas_call(
        _mods_kernel,
        grid=(n_out // tn,),
        in_specs=[
            pl.BlockSpec((MOD_ROWS, D_MODEL), lambda j: (0, 0)),
            pl.BlockSpec((D_MODEL, tn), lambda j: (0, j)),
            pl.BlockSpec((1, tn), lambda j: (0, j)),
        ],
        out_specs=pl.BlockSpec((MOD_ROWS, tn), lambda j: (0, j)),
        out_shape=jax.ShapeDtypeStruct((MOD_ROWS, n_out), F32),
        compiler_params=pltpu.CompilerParams(
            dimension_semantics=("arbitrary",), vmem_limit_bytes=VMEM_LIMIT),
        name="mods",
    )(cc, w_ada, b_ada)


N_FF_CHUNKS = D_FF // FF_TILE


def _modulated(x, g_ref, m_ref, k0):
    y = _rmsnorm(x, g_ref[...])
    return (y * (1 + _mod_slice(m_ref, k0 + 1)) + _mod_slice(m_ref, k0)).astype(BF16)


def _swiglu(xn_ref, xnn_ref, h_ref, w1_ref, w2_ref, between):
    for s in range(N_FF_CHUNKS):
        cols = slice(s * FF_TILE, (s + 1) * FF_TILE)
        gate = _dot(xn_ref[...], w1_ref[:, cols])
        up = _dot(xn_ref[...], w1_ref[:, D_FF + s * FF_TILE:D_FF + (s + 1) * FF_TILE])
        h_ref[:, cols] = (jax.nn.silu(gate) * up).astype(BF16)
        between(s)
    xn_ref[...] = xnn_ref[...]
    return _dot(h_ref[...], w2_ref[...])


def _pieces(tm, n):
    piece = -(-tm // (n * BF16_ROWS)) * BF16_ROWS
    return [(min(s * piece, tm - piece), piece) for s in range(-(-tm // piece))]


def _ffn_kernel(x_ref, xnext_ref, mod_ref, modn_ref, g_ref, w1_ref, w2_ref, *rest, n_cast):
    rest = list(rest)
    cast_in = [rest.pop(0) for _ in range(n_cast)]
    o_ref = rest.pop(0)
    cast_out = [rest.pop(0) for _ in range(n_cast)]
    xn_ref, xnn_ref, h_ref = rest
    i = pl.program_id(0)
    pieces = _pieces(x_ref.shape[0], N_FF_CHUNKS)

    @pl.when(i == 0)
    def _():
        xn_ref[...] = _modulated(x_ref[...], g_ref, mod_ref, 0)

    def between(s):
        if s < len(pieces):
            r0, n = pieces[s]
            xnn_ref[r0:r0 + n, :] = _modulated(xnext_ref[r0:r0 + n, :], g_ref, modn_ref, 0)
        if s < n_cast:
            cast_out[s][...] = cast_in[s][...].astype(BF16)

    acc = _swiglu(xn_ref, xnn_ref, h_ref, w1_ref, w2_ref, between)
    o_ref[...] = x_ref[...] + 0.5 * _mod_slice(mod_ref, 2) * acc


def _ffn_call(xr, mods3, g, w1, w2, *, tiles_per_batch, mod_row0, cast_weights=()):
    rows = xr.shape[0]
    tm = min(FFN_ROW_TILE, rows)
    n_tiles = rows // tm
    n_slabs = n_tiles // 2
    cast_specs, cast_shapes = [], []
    for w in cast_weights:
        slab = w.shape[0] // n_slabs
        assert slab * n_slabs == w.shape[0] and slab % BF16_ROWS == 0, (w.shape, n_slabs)
        cast_specs.append(pl.BlockSpec((slab, w.shape[1]), lambda i: (i // 2, 0)))
        cast_shapes.append(jax.ShapeDtypeStruct(w.shape, BF16))
    tpb = tiles_per_batch * (ROW_TILE // tm)
    nxt = lambda i: jnp.minimum(i + 1, n_tiles - 1)
    resident = dict(pipeline_mode=pl.Buffered(1))
    in_specs = [
        pl.BlockSpec((tm, D_MODEL), lambda i: (i, 0)),
        pl.BlockSpec((tm, D_MODEL), lambda i: (nxt(i), 0)),
        pl.BlockSpec((1, 1, N_MOD * D_MODEL), lambda i: (mod_row0 + i // tpb, 0, 0)),
        pl.BlockSpec((1, 1, N_MOD * D_MODEL), lambda i: (mod_row0 + nxt(i) // tpb, 0, 0)),
        pl.BlockSpec((1, D_MODEL), lambda i: (0, 0)),
        pl.BlockSpec(w1.shape, lambda i: (0, 0), **resident),
        pl.BlockSpec(w2.shape, lambda i: (0, 0), **resident),
    ]
    args = [xr, xr, mods3, mods3, g, w1, w2]
    outs = pl.pallas_call(
        functools.partial(_ffn_kernel, n_cast=len(cast_weights)),
        grid=(n_tiles,),
        in_specs=in_specs + cast_specs,
        out_specs=[pl.BlockSpec((tm, D_MODEL), lambda i: (i, 0))] + cast_specs,
        out_shape=[jax.ShapeDtypeStruct((rows, D_MODEL), F32)] + cast_shapes,
        scratch_shapes=[pltpu.VMEM((tm, D_MODEL), BF16), pltpu.VMEM((tm, D_MODEL), BF16),
                        pltpu.VMEM((tm, D_FF), BF16)],
        compiler_params=pltpu.CompilerParams(
            dimension_semantics=("arbitrary",), vmem_limit_bytes=VMEM_LIMIT),
        name="ffn",
    )(*args, *cast_weights)
    return outs[0] if not cast_weights else outs


_PROJ_COL = {"q": 0, "k": 1, "v": 2, "g": 3, "u": 4, "vg": 5}
_PROJ_ORDER = ("vg", "k", "q", "u", "g", "v")


def _proj_kernel(x_ref, mod_ref, g_ref, w_ref, *rest, outs, rope):
    if rope:
        cs_ref, sn_ref = rest[:2]
        rest = rest[2:]
    o_refs = dict(zip(outs, rest))
    y = _rmsnorm(x_ref[...], g_ref[...])
    xn = (y * (1 + _mod_slice(mod_ref, 4)) + _mod_slice(mod_ref, 3)).astype(BF16)
    k_scale = HEAD_DIM ** -0.5
    for name in sorted(outs, key=_PROJ_ORDER.index):
        col = _PROJ_COL[name] * RET_WIDTH
        p = _dot(xn, w_ref[:, col:col + RET_WIDTH])
        o_ref = o_refs[name]
        if name in ("q", "k", "v", "g"):
            for h in range(HEADS):
                t = p[:, h * HEAD_DIM:(h + 1) * HEAD_DIM]
                if name in ("q", "k") and rope:
                    t = t * cs_ref[...] + pltpu.roll(t, HEAD_DIM // 2, 1) * sn_ref[...]
                if name == "k":
                    t = t * k_scale
                if name == "g":
                    t = jax.nn.silu(t)
                if name == "k":
                    for cc in range(t.shape[0] // CHUNK):
                        o_ref[h, cc] = t[cc * CHUNK:(cc + 1) * CHUNK, :].T.astype(BF16)
                else:
                    o_ref[h] = t.astype(BF16)
        elif name == "u":
            gu = jax.nn.gelu(p)
            for gi in range(GROUPS):
                o_ref[gi] = gu[:, gi * GROUP_DIM:(gi + 1) * GROUP_DIM].astype(BF16)
        else:
            gv = jax.nn.gelu(p)
            for gi in range(GROUPS):
                t = gv[:, gi * GROUP_DIM:(gi + 1) * GROUP_DIM]
                mu = jnp.mean(t, axis=-1, keepdims=True)
                d = t - mu
                var = jnp.mean(d * d, axis=-1, keepdims=True)
                o_ref[gi] = (d * lax.rsqrt(var + EPS)).astype(BF16)


def _proj_call(xr, mods3, g, w_in, *, outs, tiles_per_batch, mod_row0, rope_tabs=None):
    rows = xr.shape[0]
    tm = ROW_TILE
    mod_map = lambda i: (mod_row0 + i // tiles_per_batch, 0, 0)
    in_specs = [
        pl.BlockSpec((tm, D_MODEL), lambda i: (i, 0)),
        pl.BlockSpec((1, 1, N_MOD * D_MODEL), mod_map),
        pl.BlockSpec((1, D_MODEL), lambda i: (0, 0)),
        pl.BlockSpec(w_in.shape, lambda i: (0, 0)),
    ]
    args = [xr, mods3, g, w_in]
    if rope_tabs is not None:
        tab_map = lambda i: (i % tiles_per_batch, 0)
        in_specs += [pl.BlockSpec((tm, HEAD_DIM), tab_map)] * 2
        args += list(rope_tabs)
    out_specs, out_shape = [], []
    for name in outs:
        if name == "k":
            out_specs.append(pl.BlockSpec((HEADS, tm // CHUNK, HEAD_DIM, CHUNK), lambda i: (0, i, 0, 0)))
            out_shape.append(jax.ShapeDtypeStruct((HEADS, rows // CHUNK, HEAD_DIM, CHUNK), BF16))
        elif name in ("q", "v", "g"):
            out_specs.append(pl.BlockSpec((HEADS, tm, HEAD_DIM), lambda i: (0, i, 0)))
            out_shape.append(jax.ShapeDtypeStruct((HEADS, rows, HEAD_DIM), BF16))
        else:
            out_specs.append(pl.BlockSpec((GROUPS, tm, GROUP_DIM), lambda i: (0, i, 0)))
            out_shape.append(jax.ShapeDtypeStruct((GROUPS, rows, GROUP_DIM), BF16))
    return pl.pallas_call(
        functools.partial(_proj_kernel, outs=outs, rope=rope_tabs is not None),
        grid=(rows // tm,),
        in_specs=in_specs,
        out_specs=out_specs,
        out_shape=out_shape,
        compiler_params=pltpu.CompilerParams(
            dimension_semantics=("parallel",), vmem_limit_bytes=VMEM_LIMIT),
        name="proj_x" if rope_tabs is not None else "proj_ctx",
    )(*args)


def _ret_kernel(rd_ref, ktc_ref, vc_ref, q_ref, kt_ref, v_ref, sg_ref, gu_ref, vn_ref, ws_ref, bs_ref,
                o_ref, gm_ref, u_ref, s_ref, *, n_chunks, n_ctx_chunks):
    L = CHUNK
    lg_f = -jnp.exp(rd_ref[0, 0, 0:1, :])
    lg_b = -jnp.exp(rd_ref[1, 0, 0:1, :])
    ii = lax.broadcasted_iota(jnp.int32, (L, L), 0).astype(F32)
    jj = lax.broadcasted_iota(jnp.int32, (L, L), 1).astype(F32)
    diff = ii - jj
    lower = diff >= 0
    upper = diff <= 0
    mask = (jnp.where(lower, jnp.exp(jnp.where(lower, diff, 0.0) * lg_f), 0.0)
            + jnp.where(upper, jnp.exp(jnp.where(upper, -diff, 0.0) * lg_b), 0.0))
    xi_f = jnp.exp((ii + 1.0) * lg_f)
    ze_f = jnp.exp((L - 1.0 - ii) * lg_f)
    xi_b = jnp.exp((L - ii) * lg_b)
    ze_b = jnp.exp(ii * lg_b)
    cd_f = jnp.exp(L * lg_f)
    cd_b = jnp.exp(L * lg_b)

    def kv_pair(kt, v):
        vf = v.astype(F32)
        vz = jnp.concatenate([(vf * ze_f).astype(BF16), (vf * ze_b).astype(BF16)], axis=1)
        return _dot(kt, vz)

    def chunk_rows(c):
        return pl.ds(pl.multiple_of(c * L, L), L)

    uc = [kv_pair(ktc_ref[0, c], vc_ref[0, c * L:(c + 1) * L, :]) for c in range(n_ctx_chunks)]
    s_f = jnp.zeros((HEAD_DIM, HEAD_DIM), F32)
    for c in range(n_ctx_chunks):
        s_f = cd_f * s_f + uc[c][:, :HEAD_DIM]
    s_b = jnp.zeros((HEAD_DIM, HEAD_DIM), F32)
    for c in reversed(range(n_ctx_chunks)):
        s_b = cd_b * s_b + uc[c][:, HEAD_DIM:]

    def u_body(c, carry):
        u_ref[c] = kv_pair(kt_ref[0, c], v_ref[0, chunk_rows(c), :])
        return carry

    lax.fori_loop(0, n_chunks, u_body, 0, unroll=RET_UNROLL)

    def f_body(c, s):
        s_ref[c, 0:HEAD_DIM, :] = s.astype(BF16)
        return cd_f * s + u_ref[c, :, 0:HEAD_DIM]

    lax.fori_loop(0, n_chunks, f_body, s_f, unroll=RET_UNROLL)

    def b_body(t, s):
        c = n_chunks - 1 - t
        s_ref[c, HEAD_DIM:2 * HEAD_DIM, :] = s.astype(BF16)
        return cd_b * s + u_ref[c, :, HEAD_DIM:2 * HEAD_DIM]

    lax.fori_loop(0, n_chunks, b_body, s_b, unroll=RET_UNROLL)

    def o_body(c, carry):
        rows = chunk_rows(c)
        q = q_ref[0, rows, :]
        qf = q.astype(F32)
        p = _dot(q, kt_ref[0, c])
        lhs = jnp.concatenate(
            [(p * mask).astype(BF16), (qf * xi_f).astype(BF16), (qf * xi_b).astype(BF16)], axis=1)
        rhs = jnp.concatenate([v_ref[0, rows, :], s_ref[c]], axis=0)
        o = _dot(lhs, rhs)
        y = o * lax.rsqrt(jnp.mean(o * o, axis=-1, keepdims=True) + EPS)
        o_ref[0, rows, :] = (y * sg_ref[0, rows, :].astype(F32)).astype(BF16)
        return carry

    lax.fori_loop(0, n_chunks, o_body, 0, unroll=RET_UNROLL)

    bias = jnp.broadcast_to(bs_ref[0], (L, GROUP_DIM))

    def g_body(c, carry):
        rows = chunk_rows(c)
        mixed = _dot(ws_ref[0], vn_ref[0, rows, :]) + bias
        gm_ref[0, rows, :] = (gu_ref[0, rows, :].astype(F32) * mixed).astype(BF16)
        return carry

    lax.fori_loop(0, n_chunks, g_body, 0, unroll=GMLP_UNROLL)


def _ret_call(rd, ktc, vc, q, kt, v, sg, gu, vn, ws, bs, *, batch, seq, ctx_len):
    assert HEADS == GROUPS and HEAD_DIM == GROUP_DIM
    n_chunks = seq // CHUNK
    n_ctx_chunks = ctx_len // CHUNK
    x_spec = pl.BlockSpec((1, seq, HEAD_DIM), lambda b, h: (h, b, 0))
    ws_spec = pl.BlockSpec((1,) + ws.shape[1:], lambda b, h: (h, 0, 0))
    bs_spec = pl.BlockSpec((1,) + bs.shape[1:], lambda b, h: (h, 0, 0))
    out_sds = jax.ShapeDtypeStruct((HEADS, batch * seq, HEAD_DIM), BF16)
    kt_spec = pl.BlockSpec((1, n_chunks, HEAD_DIM, CHUNK), lambda b, h: (h, b, 0, 0))
    vc_spec = pl.BlockSpec((1, ctx_len, HEAD_DIM), lambda b, h: (h, b, 0))
    ktc_spec = pl.BlockSpec((1, n_ctx_chunks, HEAD_DIM, CHUNK), lambda b, h: (h, b, 0, 0))
    return pl.pallas_call(
        functools.partial(_ret_kernel, n_chunks=n_chunks, n_ctx_chunks=n_ctx_chunks),
        grid=(batch, HEADS),
        in_specs=[pl.BlockSpec((2, 1, 8, HEAD_DIM), lambda b, h: (0, h, 0, 0)),
                  ktc_spec, vc_spec, x_spec, kt_spec, x_spec, x_spec, x_spec, x_spec, ws_spec, bs_spec],
        out_specs=[x_spec, x_spec],
        out_shape=[out_sds, out_sds],
        scratch_shapes=[pltpu.VMEM((n_chunks, HEAD_DIM, 2 * HEAD_DIM), F32),
                        pltpu.VMEM((n_chunks, 2 * HEAD_DIM, HEAD_DIM), BF16)],
        compiler_params=pltpu.CompilerParams(
            dimension_semantics=("parallel", "parallel"), vmem_limit_bytes=VMEM_LIMIT),
        name="retention",
    )(rd, ktc, vc, q, kt, v, sg, gu, vn, ws, bs)


def _mix_ffn_kernel(x1n_ref, retn_ref, gmn_ref, x10_ref, ret0_ref, gm0_ref,
                    mod_ref, modn_ref, wo_ref, g_ref, w1_ref, w2_ref, fg_ref,
                    o_ref, xn_ref, xnn_ref, h_ref, x2_ref):
    i = pl.program_id(0)
    tm = o_ref.shape[0]

    def mixed(x1_ref, ret_ref, gm_ref, m_ref):
        mix = jnp.concatenate([ret_ref[h] for h in range(HEADS)]
                              + [gm_ref[gi] for gi in range(GROUPS)], axis=1)
        return x1_ref[...] + _mod_slice(m_ref, 5) * _dot(mix, wo_ref[...])

    @pl.when(i == 0)
    def _():
        x2 = mixed(x10_ref, ret0_ref, gm0_ref, mod_ref)
        x2_ref[...] = x2
        xn_ref[...] = _modulated(x2, g_ref, mod_ref, 6)

    pieces = _pieces(tm, N_FF_CHUNKS - 1)
    nxt = {}

    def between(s):
        if s == 0:
            nxt["x2"] = mixed(x1n_ref, retn_ref, gmn_ref, modn_ref)
        elif s <= len(pieces):
            r0, n = pieces[s - 1]
            xnn_ref[r0:r0 + n, :] = _modulated(nxt["x2"][r0:r0 + n, :], g_ref, modn_ref, 6)

    acc = _swiglu(xn_ref, xnn_ref, h_ref, w1_ref, w2_ref, between)
    out = x2_ref[...] + 0.5 * _mod_slice(mod_ref, 8) * acc
    o_ref[...] = _rmsnorm(out, fg_ref[...])
    x2_ref[...] = nxt["x2"]


def _mix_ffn_call(x1, mods3, ret, gm, w_out, g, w1, w2, final_g, *, tiles_per_batch):
    rows = x1.shape[0]
    tm = FFN_ROW_TILE
    n_tiles = rows // tm
    tpb = tiles_per_batch * (ROW_TILE // tm)
    nxt = lambda i: jnp.minimum(i + 1, n_tiles - 1)
    first = lambda i: 0
    resident = dict(pipeline_mode=pl.Buffered(1))

    def tile_specs(idx, **kw):
        return [pl.BlockSpec((tm, D_MODEL), lambda i: (idx(i), 0), **kw),
                pl.BlockSpec((HEADS, tm, HEAD_DIM), lambda i: (0, idx(i), 0), **kw),
                pl.BlockSpec((GROUPS, tm, GROUP_DIM), lambda i: (0, idx(i), 0), **kw)]

    in_specs = tile_specs(nxt) + tile_specs(first, **resident) + [
        pl.BlockSpec((1, 1, N_MOD * D_MODEL), lambda i: (i // tpb, 0, 0)),
        pl.BlockSpec((1, 1, N_MOD * D_MODEL), lambda i: (nxt(i) // tpb, 0, 0)),
        pl.BlockSpec(w_out.shape, lambda i: (0, 0), **resident),
        pl.BlockSpec((1, D_MODEL), lambda i: (0, 0)),
        pl.BlockSpec(w1.shape, lambda i: (0, 0), **resident),
        pl.BlockSpec(w2.shape, lambda i: (0, 0), **resident),
        pl.BlockSpec((1, D_MODEL), lambda i: (0, 0)),
    ]
    tile_args = [x1, ret, gm]
    return pl.pallas_call(
        _mix_ffn_kernel,
        grid=(n_tiles,),
        in_specs=in_specs,
        out_specs=pl.BlockSpec((tm, D_MODEL), lambda i: (i, 0)),
        out_shape=jax.ShapeDtypeStruct((rows, D_MODEL), F32),
        scratch_shapes=[pltpu.VMEM((tm, D_MODEL), BF16), pltpu.VMEM((tm, D_MODEL), BF16),
                        pltpu.VMEM((tm, D_FF), BF16), pltpu.VMEM((tm, D_MODEL), F32)],
        compiler_params=pltpu.CompilerParams(
            dimension_semantics=("arbitrary",), vmem_limit_bytes=VMEM_LIMIT),
        name="mix_ffn",
    )(*tile_args, *tile_args, mods3, mods3, w_out, g, w1, w2, final_g)


def _rope_tables(seq):
    rows = seq // GRID_W
    row = np.repeat(np.arange(rows, dtype=np.float64), GRID_W)
    col = np.tile(np.arange(GRID_W, dtype=np.float64), rows)
    n_freq = HEAD_DIM // 4
    freqs = ROPE_BASE ** (-np.arange(n_freq, dtype=np.float64) / n_freq)
    ang = np.concatenate([row[:, None] * freqs, col[:, None] * freqs], axis=-1)
    cos, sin = np.cos(ang), np.sin(ang)
    return (jnp.asarray(np.concatenate([cos, cos], axis=-1), F32),
            jnp.asarray(np.concatenate([-sin, sin], axis=-1), F32))


def kernel(x, c, ctx, c_ctx, w_ada, b_ada, norm_g, ffn1_w1, ffn1_w2, w_in, ret_decay,
           gmlp_ws, gmlp_bs, w_out, ffn2_w1, ffn2_w2, final_g):
    batch, seq, d = x.shape
    ctx_len = ctx.shape[1]
    assert w_ada.shape[0] == 1, "single trunk layer only"
    assert d == D_MODEL and seq % ROW_TILE == 0 and (batch * ctx_len) % ROW_TILE == 0
    assert batch + 1 <= MOD_ROWS
    tiles_per_batch = seq // ROW_TILE

    xr = x.reshape(batch * seq, d)
    ctxr = ctx.reshape(batch * ctx_len, d)
    cc = jnp.concatenate([c, c_ctx[None], jnp.zeros((MOD_ROWS - batch - 1, d), F32)], axis=0)
    mods3 = _mods_call(cc, w_ada[0], b_ada[0][None]).reshape(MOD_ROWS, 1, N_MOD * d)

    bf = lambda w: w.astype(BF16)
    g0, g1, g2 = norm_g[0, 0][None], norm_g[0, 1][None], norm_g[0, 2][None]
    w1a, w2a = bf(ffn1_w1[0]), bf(ffn1_w2[0])
    x1, w_in_b, w_out_b, w1b, w2b = _ffn_call(
        xr, mods3, g0, w1a, w2a, tiles_per_batch=tiles_per_batch, mod_row0=0,
        cast_weights=(w_in[0], w_out[0], ffn2_w1[0], ffn2_w2[0]))
    c1 = _ffn_call(ctxr, mods3, g0, w1a, w2a,
                   tiles_per_batch=(batch * ctx_len) // ROW_TILE, mod_row0=batch)

    q, kt, v, sg, gu, vn = _proj_call(
        x1, mods3, g1, w_in_b, outs=("q", "k", "v", "g", "u", "vg"),
        tiles_per_batch=tiles_per_batch, mod_row0=0, rope_tabs=_rope_tables(seq))
    ktc, vc = _proj_call(c1, mods3, g1, w_in_b, outs=("k", "v"),
                        tiles_per_batch=(batch * ctx_len) // ROW_TILE, mod_row0=batch)

    rd = jnp.broadcast_to(ret_decay[0].astype(F32)[:, :, None, None], (2, HEADS, 8, HEAD_DIM))
    bs = gmlp_bs[0][:, :, None]
    ret, gm = _ret_call(rd, ktc, vc, q, kt, v, sg, gu, vn, bf(gmlp_ws[0]), bs,
                        batch=batch, seq=seq, ctx_len=ctx_len)
    out = _mix_ffn_call(x1, mods3, ret, gm, w_out_b, g2, w1b, w2b, final_g[None],
                        tiles_per_batch=tiles_per_batch)
    return out.reshape(batch, seq, d)
```

```python
import functools

import jax
import jax.numpy as jnp
import numpy as np
from jax import lax
from jax.experimental import pallas as pl
from jax.experimental.pallas import tpu as pltpu

F32 = jnp.float32
BF16 = jnp.bfloat16

D_MODEL = 1024
CHUNK = 128
HEADS = 4
HEAD_DIM = 128
GROUPS = 4
GROUP_DIM = 128
RET_WIDTH = HEADS * HEAD_DIM
GMLP_WIDTH = GROUPS * GROUP_DIM
D_FF = 2816
N_MOD = 9
GRID_W = 64
ROPE_BASE = 10000.0
EPS = 1e-6

ROW_TILE = 1024
FF_TILE = 256
BF16_ROWS = 16
MOD_ROWS = 8
RET_UNROLL = 32
VMEM_LIMIT = 56 * 1024 * 1024


def _rmsnorm(x, g):
    return x * lax.rsqrt(jnp.mean(x * x, axis=-1, keepdims=True) + EPS) * g


def _mod_slice(mod_ref, k):
    return mod_ref[0, :, k * D_MODEL:(k + 1) * D_MODEL]


def _dot(a, b):
    return jnp.dot(a, b, preferred_element_type=F32)


def _mods_kernel(c_ref, w_ref, b_ref, o_ref):
    s = jax.nn.silu(c_ref[...]).astype(BF16)
    o_ref[...] = _dot(s, w_ref[...].astype(BF16)) + b_ref[...]


def _mods_call(cc, w_ada, b_ada):
    n_out = w_ada.shape[1]
    tn = D_MODEL
    return pl.pallas_call(
        _mods_kernel,
        grid=(n_out // tn,),
        in_specs=[
            pl.BlockSpec((MOD_ROWS, D_MODEL), lambda j: (0, 0)),
            pl.BlockSpec((D_MODEL, tn), lambda j: (0, j)),
            pl.BlockSpec((1, tn), lambda j: (0, j)),
        ],
        out_specs=pl.BlockSpec((MOD_ROWS, tn), lambda j: (0, j)),
        out_shape=jax.ShapeDtypeStruct((MOD_ROWS, n_out), F32),
        compiler_params=pltpu.CompilerParams(
            dimension_semantics=("arbitrary",), vmem_limit_bytes=VMEM_LIMIT),
        name="mods",
    )(cc, w_ada, b_ada)


def _ffn_kernel(x_ref, xnext_ref, mod_ref, modn_ref, g_ref, w1_ref, w2_ref, *rest,
                k0, final_norm, n_cast):
    rest = list(rest)
    fg_ref = rest.pop(0) if final_norm else None
    cast_in = [rest.pop(0) for _ in range(n_cast)]
    o_ref = rest.pop(0)
    cast_out = [rest.pop(0) for _ in range(n_cast)]
    (xn_ref,) = rest
    i = pl.program_id(0)
    slot = i % 2
    tm = x_ref.shape[0]

    def normalized(x, m_ref):
        y = _rmsnorm(x, g_ref[...])
        return (y * (1 + _mod_slice(m_ref, k0 + 1)) + _mod_slice(m_ref, k0)).astype(BF16)

    @pl.when(i == 0)
    def _():
        xn_ref[0] = normalized(x_ref[...], mod_ref)

    n_sub = D_FF // FF_TILE
    piece = -(-tm // (n_sub * BF16_ROWS)) * BF16_ROWS
    acc = None
    for s in range(n_sub):
        cols = slice(s * FF_TILE, (s + 1) * FF_TILE)
        gate = _dot(xn_ref[slot], w1_ref[:, cols])
        up = _dot(xn_ref[slot], w1_ref[:, D_FF + s * FF_TILE:D_FF + (s + 1) * FF_TILE])
        h = (jax.nn.silu(gate) * up).astype(BF16)
        d = _dot(h, w2_ref[cols, :])
        acc = d if acc is None else acc + d
        r0 = min(s * piece, tm - piece)
        xn_ref[1 - slot, r0:r0 + piece, :] = normalized(xnext_ref[r0:r0 + piece, :], modn_ref)
        if s < len(cast_in):
            cast_out[s][...] = cast_in[s][...].astype(BF16)

    out = x_ref[...] + 0.5 * _mod_slice(mod_ref, k0 + 2) * acc
    if final_norm:
        out = _rmsnorm(out, fg_ref[...])
    o_ref[...] = out


def _ffn_row_tile(rows, cast_weights, n_slabs_of):
    weights = 3 * D_MODEL * D_FF * 2
    for tm in (1024, 512):
        if rows % tm:
            continue
        tile = tm * D_MODEL
        slabs = sum(w.size // n_slabs_of(rows // tm) for w in cast_weights)
        need = weights + 6 * tile * 4 + 2 * tile * 2 + 4 * tile * 4 + slabs * (2 * 4 + 2 * 2)
        if need <= VMEM_LIMIT:
            return tm
    raise ValueError(f"no SwiGLU row tile fits VMEM for {rows} rows")


def _ffn_call(xr, mods3, g, w1, w2, *, k0, rows_per_mod, mod_row0, final_g=None, cast_weights=()):
    rows = xr.shape[0]
    n_slabs_of = lambda n_tiles: n_tiles // 2
    tm = _ffn_row_tile(rows, cast_weights, n_slabs_of)
    n_tiles = rows // tm
    n_slabs = n_slabs_of(n_tiles)
    cast_specs, cast_shapes = [], []
    for w in cast_weights:
        slab = w.shape[0] // n_slabs
        assert slab * n_slabs == w.shape[0] and slab % BF16_ROWS == 0, (w.shape, n_slabs)
        cast_specs.append(pl.BlockSpec((slab, w.shape[1]), lambda i: (i // 2, 0)))
        cast_shapes.append(jax.ShapeDtypeStruct(w.shape, BF16))
    tpb = rows_per_mod // tm
    nxt = lambda i: jnp.minimum(i + 1, n_tiles - 1)
    resident = dict(pipeline_mode=pl.Buffered(1))
    in_specs = [
        pl.BlockSpec((tm, D_MODEL), lambda i: (i, 0)),
        pl.BlockSpec((tm, D_MODEL), lambda i: (nxt(i), 0)),
        pl.BlockSpec((1, 1, N_MOD * D_MODEL), lambda i: (mod_row0 + i // tpb, 0, 0)),
        pl.BlockSpec((1, 1, N_MOD * D_MODEL), lambda i: (mod_row0 + nxt(i) // tpb, 0, 0)),
        pl.BlockSpec((1, D_MODEL), lambda i: (0, 0)),
        pl.BlockSpec(w1.shape, lambda i: (0, 0), **resident),
        pl.BlockSpec(w2.shape, lambda i: (0, 0), **resident),
    ]
    args = [xr, xr, mods3, mods3, g, w1, w2]
    if final_g is not None:
        in_specs.append(pl.BlockSpec((1, D_MODEL), lambda i: (0, 0)))
        args.append(final_g)
    outs = pl.pallas_call(
        functools.partial(_ffn_kernel, k0=k0, final_norm=final_g is not None, n_cast=len(cast_weights)),
        grid=(n_tiles,),
        in_specs=in_specs + cast_specs,
        out_specs=[pl.BlockSpec((tm, D_MODEL), lambda i: (i, 0))] + cast_specs,
        out_shape=[jax.ShapeDtypeStruct((rows, D_MODEL), F32)] + cast_shapes,
        scratch_shapes=[pltpu.VMEM((2, tm, D_MODEL), BF16)],
        compiler_params=pltpu.CompilerParams(
            dimension_semantics=("arbitrary",), vmem_limit_bytes=VMEM_LIMIT),
        name="ffn_final" if final_g is not None else "ffn",
    )(*args, *cast_weights)
    return outs[0] if not cast_weights else outs


_PROJ_COL = {"q": 0, "k": 1, "v": 2, "g": 3, "u": 4, "vg": 5}
_PROJ_ORDER = ("vg", "k", "q", "u", "g", "v")


def _proj_kernel(x_ref, mod_ref, g_ref, w_ref, *rest, outs, rope):
    if rope:
        cs_ref, sn_ref = rest[:2]
        rest = rest[2:]
    o_refs = dict(zip(outs, rest))
    y = _rmsnorm(x_ref[...], g_ref[...])
    xn = (y * (1 + _mod_slice(mod_ref, 4)) + _mod_slice(mod_ref, 3)).astype(BF16)
    k_scale = HEAD_DIM ** -0.5
    for name in sorted(outs, key=_PROJ_ORDER.index):
        col = _PROJ_COL[name] * RET_WIDTH
        p = _dot(xn, w_ref[:, col:col + RET_WIDTH])
        o_ref = o_refs[name]
        if name in ("q", "k", "v", "g"):
            for h in range(HEADS):
                t = p[:, h * HEAD_DIM:(h + 1) * HEAD_DIM]
                if name in ("q", "k") and rope:
                    t = t * cs_ref[...] + pltpu.roll(t, HEAD_DIM // 2, 1) * sn_ref[...]
                if name == "k":
                    t = t * k_scale
                if name == "g":
                    t = jax.nn.silu(t)
                if name == "k":
                    for cc in range(t.shape[0] // CHUNK):
                        o_ref[h, cc] = t[cc * CHUNK:(cc + 1) * CHUNK, :].T.astype(BF16)
                else:
                    o_ref[h] = t.astype(BF16)
        elif name == "u":
            o_ref[...] = jax.nn.gelu(p).astype(BF16)
        else:
            gv = jax.nn.gelu(p)
            for gi in range(GROUPS):
                t = gv[:, gi * GROUP_DIM:(gi + 1) * GROUP_DIM]
                mu = jnp.mean(t, axis=-1, keepdims=True)
                d = t - mu
                var = jnp.mean(d * d, axis=-1, keepdims=True)
                o_ref[:, gi * GROUP_DIM:(gi + 1) * GROUP_DIM] = (d * lax.rsqrt(var + EPS)).astype(BF16)


def _proj_call(xr, mods3, g, w_in, *, outs, tiles_per_batch, mod_row0, rope_tabs=None):
    rows = xr.shape[0]
    tm = ROW_TILE
    mod_map = lambda i: (mod_row0 + i // tiles_per_batch, 0, 0)
    in_specs = [
        pl.BlockSpec((tm, D_MODEL), lambda i: (i, 0)),
        pl.BlockSpec((1, 1, N_MOD * D_MODEL), mod_map),
        pl.BlockSpec((1, D_MODEL), lambda i: (0, 0)),
        pl.BlockSpec(w_in.shape, lambda i: (0, 0)),
    ]
    args = [xr, mods3, g, w_in]
    if rope_tabs is not None:
        tab_map = lambda i: (i % tiles_per_batch, 0)
        in_specs += [pl.BlockSpec((tm, HEAD_DIM), tab_map)] * 2
        args += list(rope_tabs)
    out_specs, out_shape = [], []
    for name in outs:
        if name == "k":
            out_specs.append(pl.BlockSpec((HEADS, tm // CHUNK, HEAD_DIM, CHUNK), lambda i: (0, i, 0, 0)))
            out_shape.append(jax.ShapeDtypeStruct((HEADS, rows // CHUNK, HEAD_DIM, CHUNK), BF16))
        elif name in ("q", "v", "g"):
            out_specs.append(pl.BlockSpec((HEADS, tm, HEAD_DIM), lambda i: (0, i, 0)))
            out_shape.append(jax.ShapeDtypeStruct((HEADS, rows, HEAD_DIM), BF16))
        else:
            out_specs.append(pl.BlockSpec((tm, GMLP_WIDTH), lambda i: (i, 0)))
            out_shape.append(jax.ShapeDtypeStruct((rows, GMLP_WIDTH), BF16))
    return pl.pallas_call(
        functools.partial(_proj_kernel, outs=outs, rope=rope_tabs is not None),
        grid=(rows // tm,),
        in_specs=in_specs,
        out_specs=out_specs,
        out_shape=out_shape,
        compiler_params=pltpu.CompilerParams(
            dimension_semantics=("parallel",), vmem_limit_bytes=VMEM_LIMIT),
        name="proj_x" if rope_tabs is not None else "proj_ctx",
    )(*args)


def _ret_kernel(rd_ref, ktc_ref, vc_ref, q_ref, kt_ref, v_ref, sg_ref, o_ref, u_ref, s_ref,
                *, n_chunks, n_ctx_chunks):
    L = CHUNK
    lg_f = -jnp.exp(rd_ref[0, 0, 0:1, :])
    lg_b = -jnp.exp(rd_ref[1, 0, 0:1, :])
    ii = lax.broadcasted_iota(jnp.int32, (L, L), 0).astype(F32)
    jj = lax.broadcasted_iota(jnp.int32, (L, L), 1).astype(F32)
    diff = ii - jj
    lower = diff >= 0
    upper = diff <= 0
    mask = (jnp.where(lower, jnp.exp(jnp.where(lower, diff, 0.0) * lg_f), 0.0)
            + jnp.where(upper, jnp.exp(jnp.where(upper, -diff, 0.0) * lg_b), 0.0))
    xi_f = jnp.exp((ii + 1.0) * lg_f)
    ze_f = jnp.exp((L - 1.0 - ii) * lg_f)
    xi_b = jnp.exp((L - ii) * lg_b)
    ze_b = jnp.exp(ii * lg_b)
    cd_f = jnp.exp(L * lg_f)
    cd_b = jnp.exp(L * lg_b)

    def kv_pair(kt, v):
        vf = v.astype(F32)
        vz = jnp.concatenate([(vf * ze_f).astype(BF16), (vf * ze_b).astype(BF16)], axis=1)
        return _dot(kt, vz)

    def chunk_rows(c):
        return pl.ds(pl.multiple_of(c * L, L), L)

    uc = [kv_pair(ktc_ref[0, c], vc_ref[0, c * L:(c + 1) * L, :]) for c in range(n_ctx_chunks)]
    s_f = jnp.zeros((HEAD_DIM, HEAD_DIM), F32)
    for c in range(n_ctx_chunks):
        s_f = cd_f * s_f + uc[c][:, :HEAD_DIM]
    s_b = jnp.zeros((HEAD_DIM, HEAD_DIM), F32)
    for c in reversed(range(n_ctx_chunks)):
        s_b = cd_b * s_b + uc[c][:, HEAD_DIM:]

    def u_body(c, carry):
        u_ref[c] = kv_pair(kt_ref[0, c], v_ref[0, chunk_rows(c), :])
        return carry

    lax.fori_loop(0, n_chunks, u_body, 0, unroll=RET_UNROLL)

    def f_body(c, s):
        s_ref[c, 0:HEAD_DIM, :] = s.astype(BF16)
        return cd_f * s + u_ref[c, :, 0:HEAD_DIM]

    lax.fori_loop(0, n_chunks, f_body, s_f, unroll=RET_UNROLL)

    def b_body(t, s):
        c = n_chunks - 1 - t
        s_ref[c, HEAD_DIM:2 * HEAD_DIM, :] = s.astype(BF16)
        return cd_b * s + u_ref[c, :, HEAD_DIM:2 * HEAD_DIM]

    lax.fori_loop(0, n_chunks, b_body, s_b, unroll=RET_UNROLL)

    def o_body(c, carry):
        rows = chunk_rows(c)
        q = q_ref[0, rows, :]
        qf = q.astype(F32)
        p = _dot(q, kt_ref[0, c])
        lhs = jnp.concatenate(
            [(p * mask).astype(BF16), (qf * xi_f).astype(BF16), (qf * xi_b).astype(BF16)], axis=1)
        rhs = jnp.concatenate([v_ref[0, rows, :], s_ref[c]], axis=0)
        o = _dot(lhs, rhs)
        y = o * lax.rsqrt(jnp.mean(o * o, axis=-1, keepdims=True) + EPS)
        o_ref[0, rows, :] = (y * sg_ref[0, rows, :].astype(F32)).astype(BF16)
        return carry

    lax.fori_loop(0, n_chunks, o_body, 0, unroll=RET_UNROLL)


def _ret_call(rd, ktc, vc, q, kt, v, sg, *, batch, seq, ctx_len):
    n_chunks = seq // CHUNK
    n_ctx_chunks = ctx_len // CHUNK
    x_spec = pl.BlockSpec((1, seq, HEAD_DIM), lambda b, h: (h, b, 0))
    kt_spec = pl.BlockSpec((1, n_chunks, HEAD_DIM, CHUNK), lambda b, h: (h, b, 0, 0))
    vc_spec = pl.BlockSpec((1, ctx_len, HEAD_DIM), lambda b, h: (h, b, 0))
    ktc_spec = pl.BlockSpec((1, n_ctx_chunks, HEAD_DIM, CHUNK), lambda b, h: (h, b, 0, 0))
    return pl.pallas_call(
        functools.partial(_ret_kernel, n_chunks=n_chunks, n_ctx_chunks=n_ctx_chunks),
        grid=(batch, HEADS),
        in_specs=[pl.BlockSpec((2, 1, 8, HEAD_DIM), lambda b, h: (0, h, 0, 0)),
                  ktc_spec, vc_spec, x_spec, kt_spec, x_spec, x_spec],
        out_specs=x_spec,
        out_shape=jax.ShapeDtypeStruct((HEADS, batch * seq, HEAD_DIM), BF16),
        scratch_shapes=[pltpu.VMEM((n_chunks, HEAD_DIM, 2 * HEAD_DIM), F32),
                        pltpu.VMEM((n_chunks, 2 * HEAD_DIM, HEAD_DIM), BF16)],
        compiler_params=pltpu.CompilerParams(
            dimension_semantics=("parallel", "parallel"), vmem_limit_bytes=VMEM_LIMIT),
        name="retention",
    )(rd, ktc, vc, q, kt, v, sg)


def _mix_kernel(x_ref, mod_ref, ret_ref, gu_ref, vn_ref, ws_ref, bs_ref, wo_ref, o_ref, mix_ref):
    tm = x_ref.shape[0]
    for h in range(HEADS):
        mix_ref[:, h * HEAD_DIM:(h + 1) * HEAD_DIM] = ret_ref[h]
    for c in range(tm // CHUNK):
        rows = slice(c * CHUNK, (c + 1) * CHUNK)
        for gi in range(GROUPS):
            cols = slice(gi * GROUP_DIM, (gi + 1) * GROUP_DIM)
            mixed = _dot(ws_ref[gi], vn_ref[rows, cols]) + bs_ref[gi]
            mix_ref[rows, RET_WIDTH + gi * GROUP_DIM:RET_WIDTH + (gi + 1) * GROUP_DIM] = (
                gu_ref[rows, cols].astype(F32) * mixed).astype(BF16)
    o_ref[...] = x_ref[...] + _mod_slice(mod_ref, 5) * _dot(mix_ref[...], wo_ref[...])


def _mix_call(x1, mods3, ret, gu, vn, ws, bs, w_out, *, tiles_per_batch):
    rows = x1.shape[0]
    tm = ROW_TILE
    return pl.pallas_call(
        _mix_kernel,
        grid=(rows // tm,),
        in_specs=[
            pl.BlockSpec((tm, D_MODEL), lambda i: (i, 0)),
            pl.BlockSpec((1, 1, N_MOD * D_MODEL), lambda i: (i // tiles_per_batch, 0, 0)),
            pl.BlockSpec((HEADS, tm, HEAD_DIM), lambda i: (0, i, 0)),
            pl.BlockSpec((tm, GMLP_WIDTH), lambda i: (i, 0)),
            pl.BlockSpec((tm, GMLP_WIDTH), lambda i: (i, 0)),
            pl.BlockSpec(ws.shape, lambda i: (0, 0, 0)),
            pl.BlockSpec(bs.shape, lambda i: (0, 0, 0)),
            pl.BlockSpec(w_out.shape, lambda i: (0, 0)),
        ],
        out_specs=pl.BlockSpec((tm, D_MODEL), lambda i: (i, 0)),
        out_shape=jax.ShapeDtypeStruct((rows, D_MODEL), F32),
        scratch_shapes=[pltpu.VMEM((tm, RET_WIDTH + GMLP_WIDTH), BF16)],
        compiler_params=pltpu.CompilerParams(
            dimension_semantics=("parallel",), vmem_limit_bytes=VMEM_LIMIT),
        name="mix",
    )(x1, mods3, ret, gu, vn, ws, bs, w_out)


def _rope_tables(seq):
    rows = seq // GRID_W
    row = np.repeat(np.arange(rows, dtype=np.float64), GRID_W)
    col = np.tile(np.arange(GRID_W, dtype=np.float64), rows)
    n_freq = HEAD_DIM // 4
    freqs = ROPE_BASE ** (-np.arange(n_freq, dtype=np.float64) / n_freq)
    ang = np.concatenate([row[:, None] * freqs, col[:, None] * freqs], axis=-1)
    cos, sin = np.cos(ang), np.sin(ang)
    return (jnp.asarray(np.concatenate([cos, cos], axis=-1), F32),
            jnp.asarray(np.concatenate([-sin, sin], axis=-1), F32))


def kernel(x, c, ctx, c_ctx, w_ada, b_ada, norm_g, ffn1_w1, ffn1_w2, w_in, ret_decay,
           gmlp_ws, gmlp_bs, w_out, ffn2_w1, ffn2_w2, final_g):
    batch, seq, d = x.shape
    ctx_len = ctx.shape[1]
    assert w_ada.shape[0] == 1, "single trunk layer only"
    assert d == D_MODEL and seq % ROW_TILE == 0 and (batch * ctx_len) % ROW_TILE == 0
    assert batch + 1 <= MOD_ROWS
    tiles_per_batch = seq // ROW_TILE
    ctx_rows = batch * ctx_len

    xr = x.reshape(batch * seq, d)
    ctxr = ctx.reshape(ctx_rows, d)
    cc = jnp.concatenate([c, c_ctx[None], jnp.zeros((MOD_ROWS - batch - 1, d), F32)], axis=0)
    mods3 = _mods_call(cc, w_ada[0], b_ada[0][None]).reshape(MOD_ROWS, 1, N_MOD * d)

    bf = lambda w: w.astype(BF16)
    g0, g1, g2 = norm_g[0, 0][None], norm_g[0, 1][None], norm_g[0, 2][None]
    w1a, w2a = bf(ffn1_w1[0]), bf(ffn1_w2[0])
    x1, w_in_b, w_out_b, w1b, w2b = _ffn_call(
        xr, mods3, g0, w1a, w2a, k0=0, rows_per_mod=seq, mod_row0=0,
        cast_weights=(w_in[0], w_out[0], ffn2_w1[0], ffn2_w2[0]))
    c1 = _ffn_call(ctxr, mods3, g0, w1a, w2a, k0=0, rows_per_mod=ctx_rows, mod_row0=batch)

    q, kt, v, sg, gu, vn = _proj_call(
        x1, mods3, g1, w_in_b, outs=("q", "k", "v", "g", "u", "vg"),
        tiles_per_batch=tiles_per_batch, mod_row0=0, rope_tabs=_rope_tables(seq))
    ktc, vc = _proj_call(c1, mods3, g1, w_in_b, outs=("k", "v"),
                        tiles_per_batch=ctx_rows // ROW_TILE, mod_row0=batch)

    rd = jnp.broadcast_to(ret_decay[0].astype(F32)[:, :, None, None], (2, HEADS, 8, HEAD_DIM))
    ret = _ret_call(rd, ktc, vc, q, kt, v, sg, batch=batch, seq=seq, ctx_len=ctx_len)

    bs = gmlp_bs[0][:, :, None]
    x2 = _mix_call(x1, mods3, ret, gu, vn, bf(gmlp_ws[0]), bs, w_out_b,
                   tiles_per_batch=tiles_per_batch)
    out = _ffn_call(x2, mods3, g2, w1b, w2b, k0=6, rows_per_mod=seq, mod_row0=0,
                    final_g=final_g[None])
    return out.reshape(batch, seq, d)
```

```python
import functools

import jax
import jax.numpy as jnp
import numpy as np
from jax import lax
from jax.experimental import pallas as pl
from jax.experimental.pallas import tpu as pltpu

F32 = jnp.float32
BF16 = jnp.bfloat16

D_MODEL = 1024
CHUNK = 128
HEADS = 4
HEAD_DIM = 128
GROUPS = 4
GROUP_DIM = 128
RET_WIDTH = HEADS * HEAD_DIM
GMLP_WIDTH = GROUPS * GROUP_DIM
D_FF = 2816
N_MOD = 9
GRID_W = 64
ROPE_BASE = 10000.0
EPS = 1e-6

ROW_TILE = 1024
FF_TILE = 256
BF16_ROWS = 16
MOD_ROWS = 8
RET_UNROLL = 32
VMEM_LIMIT = 56 * 1024 * 1024


def _rmsnorm(x, g):
    return x * lax.rsqrt(jnp.mean(x * x, axis=-1, keepdims=True) + EPS) * g


def _mod_slice(mod_ref, k):
    return mod_ref[0, :, k * D_MODEL:(k + 1) * D_MODEL]


def _dot(a, b):
    return jnp.dot(a, b, preferred_element_type=F32)


def _mods_kernel(c_ref, w_ref, b_ref, o_ref):
    s = jax.nn.silu(c_ref[...]).astype(BF16)
    o_ref[...] = _dot(s, w_ref[...].astype(BF16)) + b_ref[...]


def _mods_call(cc, w_ada, b_ada):
    n_out = w_ada.shape[1]
    tn = D_MODEL
    return pl.pallas_call(
        _mods_kernel,
        grid=(n_out // tn,),
        in_specs=[
            pl.BlockSpec((MOD_ROWS, D_MODEL), lambda j: (0, 0)),
            pl.BlockSpec((D_MODEL, tn), lambda j: (0, j)),
            pl.BlockSpec((1, tn), lambda j: (0, j)),
        ],
        out_specs=pl.BlockSpec((MOD_ROWS, tn), lambda j: (0, j)),
        out_shape=jax.ShapeDtypeStruct((MOD_ROWS, n_out), F32),
        compiler_params=pltpu.CompilerParams(
            dimension_semantics=("arbitrary",), vmem_limit_bytes=VMEM_LIMIT),
        name="mods",
    )(cc, w_ada, b_ada)


def _ffn_kernel(x_ref, xnext_ref, mod_ref, modn_ref, g_ref, w1_ref, w2_ref, *rest,
                k0, final_norm, n_cast, branch_k):
    rest = list(rest)
    y_ref, ynext_ref = (rest.pop(0), rest.pop(0)) if branch_k is not None else (None, None)
    fg_ref = rest.pop(0) if final_norm else None
    cast_in = [rest.pop(0) for _ in range(n_cast)]
    o_ref = rest.pop(0)
    cast_out = [rest.pop(0) for _ in range(n_cast)]
    (xn_ref,) = rest
    i = pl.program_id(0)
    slot = i % 2
    tm = x_ref.shape[0]

    def xin(xr, yr, m_ref, rows=slice(None)):
        if branch_k is None:
            return xr[rows, :]
        return xr[rows, :] + _mod_slice(m_ref, branch_k) * yr[rows, :].astype(F32)

    def normalized(x, m_ref):
        y = _rmsnorm(x, g_ref[...])
        return (y * (1 + _mod_slice(m_ref, k0 + 1)) + _mod_slice(m_ref, k0)).astype(BF16)

    @pl.when(i == 0)
    def _():
        xn_ref[0] = normalized(xin(x_ref, y_ref, mod_ref), mod_ref)

    n_sub = D_FF // FF_TILE
    piece = -(-tm // (n_sub * BF16_ROWS)) * BF16_ROWS
    acc = None
    for s in range(n_sub):
        cols = slice(s * FF_TILE, (s + 1) * FF_TILE)
        gate = _dot(xn_ref[slot], w1_ref[:, cols])
        up = _dot(xn_ref[slot], w1_ref[:, D_FF + s * FF_TILE:D_FF + (s + 1) * FF_TILE])
        h = (jax.nn.silu(gate) * up).astype(BF16)
        d = _dot(h, w2_ref[cols, :])
        acc = d if acc is None else acc + d
        r0 = min(s * piece, tm - piece)
        xn_ref[1 - slot, r0:r0 + piece, :] = normalized(
            xin(xnext_ref, ynext_ref, modn_ref, slice(r0, r0 + piece)), modn_ref)
        if s < len(cast_in):
            cast_out[s][...] = cast_in[s][...].astype(BF16)

    out = xin(x_ref, y_ref, mod_ref) + 0.5 * _mod_slice(mod_ref, k0 + 2) * acc
    if final_norm:
        out = _rmsnorm(out, fg_ref[...])
    o_ref[...] = out


def _ffn_row_tile(rows, cast_weights, n_slabs_of):
    weights = 3 * D_MODEL * D_FF * 2
    for tm in (1024, 512):
        if rows % tm:
            continue
        tile = tm * D_MODEL
        slabs = sum(w.size // n_slabs_of(rows // tm) for w in cast_weights)
        need = weights + 6 * tile * 4 + 2 * tile * 2 + 4 * tile * 4 + slabs * (2 * 4 + 2 * 2)
        if need <= VMEM_LIMIT:
            return tm
    raise ValueError(f"no SwiGLU row tile fits VMEM for {rows} rows")


def _ffn_call(xr, mods3, g, w1, w2, *, k0, rows_per_mod, mod_row0, final_g=None, cast_weights=(),
              branch=None, branch_k=None):
    rows = xr.shape[0]
    n_slabs_of = lambda n_tiles: n_tiles // 2
    tm = _ffn_row_tile(rows, cast_weights, n_slabs_of)
    n_tiles = rows // tm
    n_slabs = n_slabs_of(n_tiles)
    cast_specs, cast_shapes = [], []
    for w in cast_weights:
        slab = w.shape[0] // n_slabs
        assert slab * n_slabs == w.shape[0] and slab % BF16_ROWS == 0, (w.shape, n_slabs)
        cast_specs.append(pl.BlockSpec((slab, w.shape[1]), lambda i: (i // 2, 0)))
        cast_shapes.append(jax.ShapeDtypeStruct(w.shape, BF16))
    tpb = rows_per_mod // tm
    nxt = lambda i: jnp.minimum(i + 1, n_tiles - 1)
    resident = dict(pipeline_mode=pl.Buffered(1))
    in_specs = [
        pl.BlockSpec((tm, D_MODEL), lambda i: (i, 0)),
        pl.BlockSpec((tm, D_MODEL), lambda i: (nxt(i), 0)),
        pl.BlockSpec((1, 1, N_MOD * D_MODEL), lambda i: (mod_row0 + i // tpb, 0, 0)),
        pl.BlockSpec((1, 1, N_MOD * D_MODEL), lambda i: (mod_row0 + nxt(i) // tpb, 0, 0)),
        pl.BlockSpec((1, D_MODEL), lambda i: (0, 0)),
        pl.BlockSpec(w1.shape, lambda i: (0, 0), **resident),
        pl.BlockSpec(w2.shape, lambda i: (0, 0), **resident),
    ]
    args = [xr, xr, mods3, mods3, g, w1, w2]
    if branch is not None:
        in_specs += [pl.BlockSpec((tm, D_MODEL), lambda i: (i, 0)),
                     pl.BlockSpec((tm, D_MODEL), lambda i: (nxt(i), 0))]
        args += [branch, branch]
    if final_g is not None:
        in_specs.append(pl.BlockSpec((1, D_MODEL), lambda i: (0, 0)))
        args.append(final_g)
    outs = pl.pallas_call(
        functools.partial(_ffn_kernel, k0=k0, final_norm=final_g is not None, n_cast=len(cast_weights),
                          branch_k=branch_k if branch is not None else None),
        grid=(n_tiles,),
        in_specs=in_specs + cast_specs,
        out_specs=[pl.BlockSpec((tm, D_MODEL), lambda i: (i, 0))] + cast_specs,
        out_shape=[jax.ShapeDtypeStruct((rows, D_MODEL), F32)] + cast_shapes,
        scratch_shapes=[pltpu.VMEM((2, tm, D_MODEL), BF16)],
        compiler_params=pltpu.CompilerParams(
            dimension_semantics=("arbitrary",), vmem_limit_bytes=VMEM_LIMIT),
        name="ffn_final" if final_g is not None else "ffn",
    )(*args, *cast_weights)
    return outs[0] if not cast_weights else outs


_PROJ_COL = {"q": 0, "k": 1, "v": 2, "g": 3, "u": 4, "vg": 5}
_PROJ_ORDER = ("vg", "k", "q", "u", "g", "v")


def _proj_kernel(x_ref, mod_ref, g_ref, w_ref, *rest, outs, rope):
    if rope:
        cs_ref, sn_ref = rest[:2]
        rest = rest[2:]
    o_refs = dict(zip(outs, rest))
    y = _rmsnorm(x_ref[...], g_ref[...])
    xn = (y * (1 + _mod_slice(mod_ref, 4)) + _mod_slice(mod_ref, 3)).astype(BF16)
    k_scale = HEAD_DIM ** -0.5
    for name in sorted(outs, key=_PROJ_ORDER.index):
        col = _PROJ_COL[name] * RET_WIDTH
        p = _dot(xn, w_ref[:, col:col + RET_WIDTH])
        o_ref = o_refs[name]
        if name in ("q", "k", "v", "g"):
            for h in range(HEADS):
                t = p[:, h * HEAD_DIM:(h + 1) * HEAD_DIM]
                if name in ("q", "k") and rope:
                    t = t * cs_ref[...] + pltpu.roll(t, HEAD_DIM // 2, 1) * sn_ref[...]
                if name == "k":
                    t = t * k_scale
                if name == "g":
                    t = jax.nn.silu(t)
                if name == "k":
                    for cc in range(t.shape[0] // CHUNK):
                        o_ref[h, cc] = t[cc * CHUNK:(cc + 1) * CHUNK, :].T.astype(BF16)
                else:
                    o_ref[h] = t.astype(BF16)
        elif name == "u":
            o_ref[...] = jax.nn.gelu(p).astype(BF16)
        else:
            gv = jax.nn.gelu(p)
            for gi in range(GROUPS):
                t = gv[:, gi * GROUP_DIM:(gi + 1) * GROUP_DIM]
                mu = jnp.mean(t, axis=-1, keepdims=True)
                d = t - mu
                var = jnp.mean(d * d, axis=-1, keepdims=True)
                o_ref[:, gi * GROUP_DIM:(gi + 1) * GROUP_DIM] = (d * lax.rsqrt(var + EPS)).astype(BF16)


def _proj_call(xr, mods3, g, w_in, *, outs, tiles_per_batch, mod_row0, rope_tabs=None):
    rows = xr.shape[0]
    tm = ROW_TILE
    mod_map = lambda i: (mod_row0 + i // tiles_per_batch, 0, 0)
    in_specs = [
        pl.BlockSpec((tm, D_MODEL), lambda i: (i, 0)),
        pl.BlockSpec((1, 1, N_MOD * D_MODEL), mod_map),
        pl.BlockSpec((1, D_MODEL), lambda i: (0, 0)),
        pl.BlockSpec(w_in.shape, lambda i: (0, 0)),
    ]
    args = [xr, mods3, g, w_in]
    if rope_tabs is not None:
        tab_map = lambda i: (i % tiles_per_batch, 0)
        in_specs += [pl.BlockSpec((tm, HEAD_DIM), tab_map)] * 2
        args += list(rope_tabs)
    out_specs, out_shape = [], []
    for name in outs:
        if name == "k":
            out_specs.append(pl.BlockSpec((HEADS, tm // CHUNK, HEAD_DIM, CHUNK), lambda i: (0, i, 0, 0)))
            out_shape.append(jax.ShapeDtypeStruct((HEADS, rows // CHUNK, HEAD_DIM, CHUNK), BF16))
        elif name in ("q", "v", "g"):
            out_specs.append(pl.BlockSpec((HEADS, tm, HEAD_DIM), lambda i: (0, i, 0)))
            out_shape.append(jax.ShapeDtypeStruct((HEADS, rows, HEAD_DIM), BF16))
        else:
            out_specs.append(pl.BlockSpec((tm, GMLP_WIDTH), lambda i: (i, 0)))
            out_shape.append(jax.ShapeDtypeStruct((rows, GMLP_WIDTH), BF16))
    return pl.pallas_call(
        functools.partial(_proj_kernel, outs=outs, rope=rope_tabs is not None),
        grid=(rows // tm,),
        in_specs=in_specs,
        out_specs=out_specs,
        out_shape=out_shape,
        compiler_params=pltpu.CompilerParams(
            dimension_semantics=("parallel",), vmem_limit_bytes=VMEM_LIMIT),
        name="proj_x" if rope_tabs is not None else "proj_ctx",
    )(*args)


def _ret_kernel(rd_ref, ktc_ref, vc_ref, q_ref, kt_ref, v_ref, sg_ref, o_ref, u_ref, s_ref,
                *, n_chunks, n_ctx_chunks):
    L = CHUNK
    lg_f = -jnp.exp(rd_ref[0, 0, 0:1, :])
    lg_b = -jnp.exp(rd_ref[1, 0, 0:1, :])
    ii = lax.broadcasted_iota(jnp.int32, (L, L), 0).astype(F32)
    jj = lax.broadcasted_iota(jnp.int32, (L, L), 1).astype(F32)
    diff = ii - jj
    lower = diff >= 0
    upper = diff <= 0
    mask = (jnp.where(lower, jnp.exp(jnp.where(lower, diff, 0.0) * lg_f), 0.0)
            + jnp.where(upper, jnp.exp(jnp.where(upper, -diff, 0.0) * lg_b), 0.0))
    xi_f = jnp.exp((ii + 1.0) * lg_f)
    ze_f = jnp.exp((L - 1.0 - ii) * lg_f)
    xi_b = jnp.exp((L - ii) * lg_b)
    ze_b = jnp.exp(ii * lg_b)
    cd_f = jnp.exp(L * lg_f)
    cd_b = jnp.exp(L * lg_b)

    def kv_pair(kt, v):
        vf = v.astype(F32)
        vz = jnp.concatenate([(vf * ze_f).astype(BF16), (vf * ze_b).astype(BF16)], axis=1)
        return _dot(kt, vz)

    def chunk_rows(c):
        return pl.ds(pl.multiple_of(c * L, L), L)

    uc = [kv_pair(ktc_ref[0, c], vc_ref[0, c * L:(c + 1) * L, :]) for c in range(n_ctx_chunks)]
    s_f = jnp.zeros((HEAD_DIM, HEAD_DIM), F32)
    for c in range(n_ctx_chunks):
        s_f = cd_f * s_f + uc[c][:, :HEAD_DIM]
    s_b = jnp.zeros((HEAD_DIM, HEAD_DIM), F32)
    for c in reversed(range(n_ctx_chunks)):
        s_b = cd_b * s_b + uc[c][:, HEAD_DIM:]

    def u_body(c, carry):
        u_ref[c] = kv_pair(kt_ref[0, c], v_ref[0, chunk_rows(c), :])
        return carry

    lax.fori_loop(0, n_chunks, u_body, 0, unroll=RET_UNROLL)

    def f_body(c, s):
        s_ref[c, 0:HEAD_DIM, :] = s.astype(BF16)
        return cd_f * s + u_ref[c, :, 0:HEAD_DIM]

    lax.fori_loop(0, n_chunks, f_body, s_f, unroll=RET_UNROLL)

    def b_body(t, s):
        c = n_chunks - 1 - t
        s_ref[c, HEAD_DIM:2 * HEAD_DIM, :] = s.astype(BF16)
        return cd_b * s + u_ref[c, :, HEAD_DIM:2 * HEAD_DIM]

    lax.fori_loop(0, n_chunks, b_body, s_b, unroll=RET_UNROLL)

    def o_body(c, carry):
        rows = chunk_rows(c)
        q = q_ref[0, rows, :]
        qf = q.astype(F32)
        p = _dot(q, kt_ref[0, c])
        lhs = jnp.concatenate(
            [(p * mask).astype(BF16), (qf * xi_f).astype(BF16), (qf * xi_b).astype(BF16)], axis=1)
        rhs = jnp.concatenate([v_ref[0, rows, :], s_ref[c]], axis=0)
        o = _dot(lhs, rhs)
        y = o * lax.rsqrt(jnp.mean(o * o, axis=-1, keepdims=True) + EPS)
        o_ref[0, rows, :] = (y * sg_ref[0, rows, :].astype(F32)).astype(BF16)
        return carry

    lax.fori_loop(0, n_chunks, o_body, 0, unroll=RET_UNROLL)


def _ret_call(rd, ktc, vc, q, kt, v, sg, *, batch, seq, ctx_len):
    n_chunks = seq // CHUNK
    n_ctx_chunks = ctx_len // CHUNK
    x_spec = pl.BlockSpec((1, seq, HEAD_DIM), lambda b, h: (h, b, 0))
    kt_spec = pl.BlockSpec((1, n_chunks, HEAD_DIM, CHUNK), lambda b, h: (h, b, 0, 0))
    vc_spec = pl.BlockSpec((1, ctx_len, HEAD_DIM), lambda b, h: (h, b, 0))
    ktc_spec = pl.BlockSpec((1, n_ctx_chunks, HEAD_DIM, CHUNK), lambda b, h: (h, b, 0, 0))
    return pl.pallas_call(
        functools.partial(_ret_kernel, n_chunks=n_chunks, n_ctx_chunks=n_ctx_chunks),
        grid=(batch, HEADS),
        in_specs=[pl.BlockSpec((2, 1, 8, HEAD_DIM), lambda b, h: (0, h, 0, 0)),
                  ktc_spec, vc_spec, x_spec, kt_spec, x_spec, x_spec],
        out_specs=x_spec,
        out_shape=jax.ShapeDtypeStruct((HEADS, batch * seq, HEAD_DIM), BF16),
        scratch_shapes=[pltpu.VMEM((n_chunks, HEAD_DIM, 2 * HEAD_DIM), F32),
                        pltpu.VMEM((n_chunks, 2 * HEAD_DIM, HEAD_DIM), BF16)],
        compiler_params=pltpu.CompilerParams(
            dimension_semantics=("parallel", "parallel"), vmem_limit_bytes=VMEM_LIMIT),
        name="retention",
    )(rd, ktc, vc, q, kt, v, sg)


def _mix_kernel(ret_ref, gu_ref, vn_ref, ws_ref, bs_ref, wo_ref, o_ref, mix_ref):
    tm = o_ref.shape[0]
    for h in range(HEADS):
        mix_ref[:, h * HEAD_DIM:(h + 1) * HEAD_DIM] = ret_ref[h]
    for c in range(tm // CHUNK):
        rows = slice(c * CHUNK, (c + 1) * CHUNK)
        for gi in range(GROUPS):
            cols = slice(gi * GROUP_DIM, (gi + 1) * GROUP_DIM)
            mixed = _dot(ws_ref[gi], vn_ref[rows, cols]) + bs_ref[gi]
            mix_ref[rows, RET_WIDTH + gi * GROUP_DIM:RET_WIDTH + (gi + 1) * GROUP_DIM] = (
                gu_ref[rows, cols].astype(F32) * mixed).astype(BF16)
    o_ref[...] = _dot(mix_ref[...], wo_ref[...]).astype(BF16)


def _mix_call(ret, gu, vn, ws, bs, w_out):
    rows = gu.shape[0]
    tm = ROW_TILE
    return pl.pallas_call(
        _mix_kernel,
        grid=(rows // tm,),
        in_specs=[
            pl.BlockSpec((HEADS, tm, HEAD_DIM), lambda i: (0, i, 0)),
            pl.BlockSpec((tm, GMLP_WIDTH), lambda i: (i, 0)),
            pl.BlockSpec((tm, GMLP_WIDTH), lambda i: (i, 0)),
            pl.BlockSpec(ws.shape, lambda i: (0, 0, 0)),
            pl.BlockSpec(bs.shape, lambda i: (0, 0, 0)),
            pl.BlockSpec(w_out.shape, lambda i: (0, 0)),
        ],
        out_specs=pl.BlockSpec((tm, D_MODEL), lambda i: (i, 0)),
        out_shape=jax.ShapeDtypeStruct((rows, D_MODEL), BF16),
        scratch_shapes=[pltpu.VMEM((tm, RET_WIDTH + GMLP_WIDTH), BF16)],
        compiler_params=pltpu.CompilerParams(
            dimension_semantics=("parallel",), vmem_limit_bytes=VMEM_LIMIT),
        name="mix",
    )(ret, gu, vn, ws, bs, w_out)


def _rope_tables(seq):
    rows = seq // GRID_W
    row = np.repeat(np.arange(rows, dtype=np.float64), GRID_W)
    col = np.tile(np.arange(GRID_W, dtype=np.float64), rows)
    n_freq = HEAD_DIM // 4
    freqs = ROPE_BASE ** (-np.arange(n_freq, dtype=np.float64) / n_freq)
    ang = np.concatenate([row[:, None] * freqs, col[:, None] * freqs], axis=-1)
    cos, sin = np.cos(ang), np.sin(ang)
    return (jnp.asarray(np.concatenate([cos, cos], axis=-1), F32),
            jnp.asarray(np.concatenate([-sin, sin], axis=-1), F32))


def kernel(x, c, ctx, c_ctx, w_ada, b_ada, norm_g, ffn1_w1, ffn1_w2, w_in, ret_decay,
           gmlp_ws, gmlp_bs, w_out, ffn2_w1, ffn2_w2, final_g):
    batch, seq, d = x.shape
    ctx_len = ctx.shape[1]
    assert w_ada.shape[0] == 1, "single trunk layer only"
    assert d == D_MODEL and seq % ROW_TILE == 0 and (batch * ctx_len) % ROW_TILE == 0
    assert batch + 1 <= MOD_ROWS
    tiles_per_batch = seq // ROW_TILE
    ctx_rows = batch * ctx_len

    xr = x.reshape(batch * seq, d)
    ctxr = ctx.reshape(ctx_rows, d)
    cc = jnp.concatenate([c, c_ctx[None], jnp.zeros((MOD_ROWS - batch - 1, d), F32)], axis=0)
    mods3 = _mods_call(cc, w_ada[0], b_ada[0][None]).reshape(MOD_ROWS, 1, N_MOD * d)

    bf = lambda w: w.astype(BF16)
    g0, g1, g2 = norm_g[0, 0][None], norm_g[0, 1][None], norm_g[0, 2][None]
    w1a, w2a = bf(ffn1_w1[0]), bf(ffn1_w2[0])
    x1, w_in_b, w_out_b, w1b, w2b = _ffn_call(
        xr, mods3, g0, w1a, w2a, k0=0, rows_per_mod=seq, mod_row0=0,
        cast_weights=(w_in[0], w_out[0], ffn2_w1[0], ffn2_w2[0]))
    c1 = _ffn_call(ctxr, mods3, g0, w1a, w2a, k0=0, rows_per_mod=ctx_rows, mod_row0=batch)

    q, kt, v, sg, gu, vn = _proj_call(
        x1, mods3, g1, w_in_b, outs=("q", "k", "v", "g", "u", "vg"),
        tiles_per_batch=tiles_per_batch, mod_row0=0, rope_tabs=_rope_tables(seq))
    ktc, vc = _proj_call(c1, mods3, g1, w_in_b, outs=("k", "v"),
                        tiles_per_batch=ctx_rows // ROW_TILE, mod_row0=batch)

    rd = jnp.broadcast_to(ret_decay[0].astype(F32)[:, :, None, None], (2, HEADS, 8, HEAD_DIM))
    ret = _ret_call(rd, ktc, vc, q, kt, v, sg, batch=batch, seq=seq, ctx_len=ctx_len)

    bs = gmlp_bs[0][:, :, None]
    y = _mix_call(ret, gu, vn, bf(gmlp_ws[0]), bs, w_out_b)
    out = _ffn_call(x1, mods3, g2, w1b, w2b, k0=6, rows_per_mod=seq, mod_row0=0,
                    final_g=final_g[None], branch=y, branch_k=5)
    return out.reshape(batch, seq, d)
```

```python
import functools

import jax
import jax.numpy as jnp
import numpy as np
from jax import lax
from jax.experimental import pallas as pl
from jax.experimental.pallas import tpu as pltpu

F32 = jnp.float32
BF16 = jnp.bfloat16

D_MODEL = 1024
CHUNK = 128
HEADS = 4
HEAD_DIM = 128
GROUPS = 4
GROUP_DIM = 128
RET_WIDTH = HEADS * HEAD_DIM
GMLP_WIDTH = GROUPS * GROUP_DIM
D_FF = 2816
N_MOD = 9
GRID_W = 64
ROPE_BASE = 10000.0
EPS = 1e-6

ROW_TILE = 1024
FF_TILE = 256
BF16_ROWS = 16
MOD_ROWS = 8
RET_UNROLL = 32
VMEM_LIMIT = 56 * 1024 * 1024


def _rmsnorm(x, g):
    return x * lax.rsqrt(jnp.mean(x * x, axis=-1, keepdims=True) + EPS) * g


def _mod_slice(mod_ref, k):
    return mod_ref[0, :, k * D_MODEL:(k + 1) * D_MODEL]


def _dot(a, b):
    return jnp.dot(a, b, preferred_element_type=F32)


def _mods_kernel(c_ref, w_ref, b_ref, o_ref):
    s = jax.nn.silu(c_ref[...]).astype(BF16)
    o_ref[...] = _dot(s, w_ref[...].astype(BF16)) + b_ref[...]


def _mods_call(cc, w_ada, b_ada):
    n_out = w_ada.shape[1]
    tn = D_MODEL
    return pl.pallas_call(
        _mods_kernel,
        grid=(n_out // tn,),
        in_specs=[
            pl.BlockSpec((MOD_ROWS, D_MODEL), lambda j: (0, 0)),
            pl.BlockSpec((D_MODEL, tn), lambda j: (0, j)),
            pl.BlockSpec((1, tn), lambda j: (0, j)),
        ],
        out_specs=pl.BlockSpec((MOD_ROWS, tn), lambda j: (0, j)),
        out_shape=jax.ShapeDtypeStruct((MOD_ROWS, n_out), F32),
        compiler_params=pltpu.CompilerParams(
            dimension_semantics=("arbitrary",), vmem_limit_bytes=VMEM_LIMIT),
        name="mods",
    )(cc, w_ada, b_ada)


N_FF_CHUNKS = D_FF // FF_TILE
FFN_ROW_TILE = 512
W_LOAD_CHUNKS = 16


def _swiglu_chunks(xn_ref, slot, w1_ref, w2_ref, between):
    acc = None
    for s in range(N_FF_CHUNKS):
        cols = slice(s * FF_TILE, (s + 1) * FF_TILE)
        gate = _dot(xn_ref[slot], w1_ref[:, cols])
        up = _dot(xn_ref[slot], w1_ref[:, D_FF + s * FF_TILE:D_FF + (s + 1) * FF_TILE])
        h = (jax.nn.silu(gate) * up).astype(BF16)
        d = _dot(h, w2_ref[cols, :])
        acc = d if acc is None else acc + d
        between(s)
    return acc


def _piece(tm, s):
    n = -(-tm // (N_FF_CHUNKS * BF16_ROWS)) * BF16_ROWS
    return min(s * n, tm - n), n


def _ffn1_kernel(x_ref, xnext_ref, c_ref, mod_ref, modn_ref, g_ref, w1_hbm, w2_hbm, *rest,
                 n_cast, n_x_tiles, n_c_tiles):
    rest = list(rest)
    cast_in = [rest.pop(0) for _ in range(n_cast)]
    o_ref = rest.pop(0)
    cast_out = [rest.pop(0) for _ in range(n_cast)]
    xn_ref, w1_ref, w2_ref, st1_ref, st2_ref, sem1, sem2 = rest
    i = pl.program_id(0)
    slot = i % 2
    tm = o_ref.shape[0]

    def tile_rows(j, lat_ref, r0, n):
        c0 = jnp.clip(j - n_x_tiles, 0, n_c_tiles - 1) * tm + r0
        ctx_rows = c_ref[pl.ds(pl.multiple_of(c0, BF16_ROWS), n), :]
        return jnp.where(j < n_x_tiles, lat_ref[r0:r0 + n, :], ctx_rows)

    def normalized(x, m_ref):
        y = _rmsnorm(x, g_ref[...])
        return (y * (1 + _mod_slice(m_ref, 1)) + _mod_slice(m_ref, 0)).astype(BF16)

    def load_weight(hbm, dst, stage, sem):
        rows = dst.shape[0] // W_LOAD_CHUNKS

        def copy(c):
            return pltpu.make_async_copy(hbm.at[pl.ds(c * rows, rows), :], stage.at[c % 2], sem.at[c % 2])

        copy(0).start()
        for c in range(W_LOAD_CHUNKS):
            if c + 1 < W_LOAD_CHUNKS:
                copy(c + 1).start()
            copy(c).wait()
            dst[c * rows:(c + 1) * rows, :] = stage[c % 2].astype(BF16)

    @pl.when(i == 0)
    def _():
        load_weight(w1_hbm, w1_ref, st1_ref, sem1)
        load_weight(w2_hbm, w2_ref, st2_ref, sem2)
        xn_ref[0] = normalized(x_ref[...], mod_ref)

    def between(s):
        r0, n = _piece(tm, s)
        xn_ref[1 - slot, r0:r0 + n, :] = normalized(tile_rows(i + 1, xnext_ref, r0, n), modn_ref)
        if s < n_cast:
            cast_out[s][...] = cast_in[s][...].astype(BF16)

    acc = _swiglu_chunks(xn_ref, slot, w1_ref, w2_ref, between)
    o_ref[...] = tile_rows(i, x_ref, 0, tm) + 0.5 * _mod_slice(mod_ref, 2) * acc


def _ffn1_call(xr, ctxr, mods3, g, w1, w2, *, rows_per_mod, ctx_mod_row, cast_weights):
    tm = FFN_ROW_TILE
    n_x, n_c = xr.shape[0] // tm, ctxr.shape[0] // tm
    assert n_x * tm == xr.shape[0] and n_c * tm == ctxr.shape[0] and rows_per_mod % tm == 0
    n_tiles = n_x + n_c
    tpb = rows_per_mod // tm
    lat = lambda j: jnp.minimum(j, n_x - 1)
    nxt = lambda i: jnp.minimum(i + 1, n_tiles - 1)
    mod_row = lambda j: jnp.where(j < n_x, j // tpb, ctx_mod_row)
    cast_specs, cast_shapes = [], []
    for w in cast_weights:
        per = 1 if (w.shape[0] // n_x) % BF16_ROWS == 0 else 2
        slab = w.shape[0] * per // n_x
        assert slab * n_x == w.shape[0] * per and slab % BF16_ROWS == 0, (w.shape, n_x)
        cast_specs.append(pl.BlockSpec((slab, w.shape[1]), lambda i, per=per: (lat(i) // per, 0)))
        cast_shapes.append(jax.ShapeDtypeStruct(w.shape, BF16))
    st1_rows, st2_rows = w1.shape[0] // W_LOAD_CHUNKS, w2.shape[0] // W_LOAD_CHUNKS
    assert st1_rows % BF16_ROWS == 0 and st2_rows % BF16_ROWS == 0
    in_specs = [
        pl.BlockSpec((tm, D_MODEL), lambda i: (lat(i), 0)),
        pl.BlockSpec((tm, D_MODEL), lambda i: (lat(i + 1), 0)),
        pl.BlockSpec(ctxr.shape, lambda i: (0, 0), pipeline_mode=pl.Buffered(1)),
        pl.BlockSpec((1, 1, N_MOD * D_MODEL), lambda i: (mod_row(i), 0, 0)),
        pl.BlockSpec((1, 1, N_MOD * D_MODEL), lambda i: (mod_row(nxt(i)), 0, 0)),
        pl.BlockSpec((1, D_MODEL), lambda i: (0, 0)),
        pl.BlockSpec(memory_space=pl.ANY),
        pl.BlockSpec(memory_space=pl.ANY),
    ]
    return pl.pallas_call(
        functools.partial(_ffn1_kernel, n_cast=len(cast_weights), n_x_tiles=n_x, n_c_tiles=n_c),
        grid=(n_tiles,),
        in_specs=in_specs + cast_specs,
        out_specs=[pl.BlockSpec((tm, D_MODEL), lambda i: (i, 0))] + cast_specs,
        out_shape=[jax.ShapeDtypeStruct((n_tiles * tm, D_MODEL), F32)] + cast_shapes,
        scratch_shapes=[pltpu.VMEM((2, tm, D_MODEL), BF16),
                        pltpu.VMEM(w1.shape, BF16), pltpu.VMEM(w2.shape, BF16),
                        pltpu.VMEM((2, st1_rows, w1.shape[1]), F32),
                        pltpu.VMEM((2, st2_rows, w2.shape[1]), F32),
                        pltpu.SemaphoreType.DMA((2,)), pltpu.SemaphoreType.DMA((2,))],
        compiler_params=pltpu.CompilerParams(
            dimension_semantics=("arbitrary",), vmem_limit_bytes=VMEM_LIMIT),
        name="ffn1",
    )(xr, xr, ctxr, mods3, mods3, g, w1, w2, *cast_weights)


def _ffn2_kernel(x_ref, xnext_ref, y_ref, ynext_ref, mod_ref, modn_ref, g_ref, w1_ref, w2_ref, fg_ref,
                 o_ref, xn_ref):
    i = pl.program_id(0)
    slot = i % 2
    tm = x_ref.shape[0]

    def x2(xr, yr, m_ref, rows=slice(None)):
        return xr[rows, :] + _mod_slice(m_ref, 5) * yr[rows, :].astype(F32)

    def normalized(x, m_ref):
        y = _rmsnorm(x, g_ref[...])
        return (y * (1 + _mod_slice(m_ref, 7)) + _mod_slice(m_ref, 6)).astype(BF16)

    @pl.when(i == 0)
    def _():
        xn_ref[0] = normalized(x2(x_ref, y_ref, mod_ref), mod_ref)

    def between(s):
        r0, n = _piece(tm, s)
        xn_ref[1 - slot, r0:r0 + n, :] = normalized(
            x2(xnext_ref, ynext_ref, modn_ref, slice(r0, r0 + n)), modn_ref)

    acc = _swiglu_chunks(xn_ref, slot, w1_ref, w2_ref, between)
    out = x2(x_ref, y_ref, mod_ref) + 0.5 * _mod_slice(mod_ref, 8) * acc
    o_ref[...] = _rmsnorm(out, fg_ref[...])


def _ffn2_call(xr, y, mods3, g, w1, w2, final_g, *, rows, rows_per_mod):
    tm = FFN_ROW_TILE
    n_tiles = rows // tm
    assert n_tiles * tm == rows and rows_per_mod % tm == 0
    tpb = rows_per_mod // tm
    nxt = lambda i: jnp.minimum(i + 1, n_tiles - 1)
    cur_spec = pl.BlockSpec((tm, D_MODEL), lambda i: (i, 0))
    nxt_spec = pl.BlockSpec((tm, D_MODEL), lambda i: (nxt(i), 0))
    vec_spec = pl.BlockSpec((1, D_MODEL), lambda i: (0, 0))
    resident = dict(pipeline_mode=pl.Buffered(1))
    return pl.pallas_call(
        _ffn2_kernel,
        grid=(n_tiles,),
        in_specs=[
            cur_spec, nxt_spec, cur_spec, nxt_spec,
            pl.BlockSpec((1, 1, N_MOD * D_MODEL), lambda i: (i // tpb, 0, 0)),
            pl.BlockSpec((1, 1, N_MOD * D_MODEL), lambda i: (nxt(i) // tpb, 0, 0)),
            vec_spec,
            pl.BlockSpec(w1.shape, lambda i: (0, 0), **resident),
            pl.BlockSpec(w2.shape, lambda i: (0, 0), **resident),
            vec_spec,
        ],
        out_specs=cur_spec,
        out_shape=jax.ShapeDtypeStruct((rows, D_MODEL), F32),
        scratch_shapes=[pltpu.VMEM((2, tm, D_MODEL), BF16)],
        compiler_params=pltpu.CompilerParams(
            dimension_semantics=("arbitrary",), vmem_limit_bytes=VMEM_LIMIT),
        name="ffn2",
    )(xr, xr, y, y, mods3, mods3, g, w1, w2, final_g)


_PROJ_COL = {"q": 0, "k": 1, "v": 2, "g": 3, "u": 4, "vg": 5}
_PROJ_ORDER = ("vg", "k", "q", "u", "g", "v")


def _proj_kernel(x_ref, mod_ref, g_ref, w_ref, *rest, outs, rope):
    if rope:
        cs_ref, sn_ref = rest[:2]
        rest = rest[2:]
    o_refs = dict(zip(outs, rest))
    y = _rmsnorm(x_ref[...], g_ref[...])
    xn = (y * (1 + _mod_slice(mod_ref, 4)) + _mod_slice(mod_ref, 3)).astype(BF16)
    k_scale = HEAD_DIM ** -0.5
    for name in sorted(outs, key=_PROJ_ORDER.index):
        col = _PROJ_COL[name] * RET_WIDTH
        p = _dot(xn, w_ref[:, col:col + RET_WIDTH])
        o_ref = o_refs[name]
        if name in ("q", "k", "v", "g"):
            for h in range(HEADS):
                t = p[:, h * HEAD_DIM:(h + 1) * HEAD_DIM]
                if name in ("q", "k") and rope:
                    t = t * cs_ref[...] + pltpu.roll(t, HEAD_DIM // 2, 1) * sn_ref[...]
                if name == "k":
                    t = t * k_scale
                if name == "g":
                    t = jax.nn.silu(t)
                if name == "k":
                    for cc in range(t.shape[0] // CHUNK):
                        o_ref[h, cc] = t[cc * CHUNK:(cc + 1) * CHUNK, :].T.astype(BF16)
                else:
                    o_ref[h] = t.astype(BF16)
        elif name == "u":
            o_ref[...] = jax.nn.gelu(p).astype(BF16)
        else:
            gv = jax.nn.gelu(p)
            for gi in range(GROUPS):
                t = gv[:, gi * GROUP_DIM:(gi + 1) * GROUP_DIM]
                mu = jnp.mean(t, axis=-1, keepdims=True)
                d = t - mu
                var = jnp.mean(d * d, axis=-1, keepdims=True)
                o_ref[:, gi * GROUP_DIM:(gi + 1) * GROUP_DIM] = (d * lax.rsqrt(var + EPS)).astype(BF16)


def _proj_call(xr, mods3, g, w_in, *, outs, row0, rows, tiles_per_batch, mod_row0, rope_tabs=None):
    tm = ROW_TILE
    assert row0 % tm == 0 and rows % tm == 0
    tile0 = row0 // tm
    mod_map = lambda i: (mod_row0 + i // tiles_per_batch, 0, 0)
    in_specs = [
        pl.BlockSpec((tm, D_MODEL), lambda i: (tile0 + i, 0)),
        pl.BlockSpec((1, 1, N_MOD * D_MODEL), mod_map),
        pl.BlockSpec((1, D_MODEL), lambda i: (0, 0)),
        pl.BlockSpec(w_in.shape, lambda i: (0, 0)),
    ]
    args = [xr, mods3, g, w_in]
    if rope_tabs is not None:
        tab_map = lambda i: (i % tiles_per_batch, 0)
        in_specs += [pl.BlockSpec((tm, HEAD_DIM), tab_map)] * 2
        args += list(rope_tabs)
    out_specs, out_shape = [], []
    for name in outs:
        if name == "k":
            out_specs.append(pl.BlockSpec((HEADS, tm // CHUNK, HEAD_DIM, CHUNK), lambda i: (0, i, 0, 0)))
            out_shape.append(jax.ShapeDtypeStruct((HEADS, rows // CHUNK, HEAD_DIM, CHUNK), BF16))
        elif name in ("q", "v", "g"):
            out_specs.append(pl.BlockSpec((HEADS, tm, HEAD_DIM), lambda i: (0, i, 0)))
            out_shape.append(jax.ShapeDtypeStruct((HEADS, rows, HEAD_DIM), BF16))
        else:
            out_specs.append(pl.BlockSpec((tm, GMLP_WIDTH), lambda i: (i, 0)))
            out_shape.append(jax.ShapeDtypeStruct((rows, GMLP_WIDTH), BF16))
    return pl.pallas_call(
        functools.partial(_proj_kernel, outs=outs, rope=rope_tabs is not None),
        grid=(rows // tm,),
        in_specs=in_specs,
        out_specs=out_specs,
        out_shape=out_shape,
        compiler_params=pltpu.CompilerParams(
            dimension_semantics=("parallel",), vmem_limit_bytes=VMEM_LIMIT),
        name="proj_x" if rope_tabs is not None else "proj_ctx",
    )(*args)


def _ret_kernel(rd_ref, ktc_ref, vc_ref, q_ref, kt_ref, v_ref, sg_ref, o_ref, u_ref, s_ref,
                *, n_chunks, n_ctx_chunks):
    L = CHUNK
    lg_f = -jnp.exp(rd_ref[0, 0, 0:1, :])
    lg_b = -jnp.exp(rd_ref[1, 0, 0:1, :])
    ii = lax.broadcasted_iota(jnp.int32, (L, L), 0).astype(F32)
    jj = lax.broadcasted_iota(jnp.int32, (L, L), 1).astype(F32)
    diff = ii - jj
    lower = diff >= 0
    upper = diff <= 0
    mask = (jnp.where(lower, jnp.exp(jnp.where(lower, diff, 0.0) * lg_f), 0.0)
            + jnp.where(upper, jnp.exp(jnp.where(upper, -diff, 0.0) * lg_b), 0.0))
    xi_f = jnp.exp((ii + 1.0) * lg_f)
    ze_f = jnp.exp((L - 1.0 - ii) * lg_f)
    xi_b = jnp.exp((L - ii) * lg_b)
    ze_b = jnp.exp(ii * lg_b)
    cd_f = jnp.exp(L * lg_f)
    cd_b = jnp.exp(L * lg_b)

    def kv_pair(kt, v):
        vf = v.astype(F32)
        vz = jnp.concatenate([(vf * ze_f).astype(BF16), (vf * ze_b).astype(BF16)], axis=1)
        return _dot(kt, vz)

    def chunk_rows(c):
        return pl.ds(pl.multiple_of(c * L, L), L)

    uc = [kv_pair(ktc_ref[0, c], vc_ref[0, c * L:(c + 1) * L, :]) for c in range(n_ctx_chunks)]
    s_f = jnp.zeros((HEAD_DIM, HEAD_DIM), F32)
    for c in range(n_ctx_chunks):
        s_f = cd_f * s_f + uc[c][:, :HEAD_DIM]
    s_b = jnp.zeros((HEAD_DIM, HEAD_DIM), F32)
    for c in reversed(range(n_ctx_chunks)):
        s_b = cd_b * s_b + uc[c][:, HEAD_DIM:]

    def u_body(c, carry):
        u_ref[c] = kv_pair(kt_ref[0, c], v_ref[0, chunk_rows(c), :])
        return carry

    lax.fori_loop(0, n_chunks, u_body, 0, unroll=RET_UNROLL)

    def f_body(c, s):
        s_ref[c, 0:HEAD_DIM, :] = s.astype(BF16)
        return cd_f * s + u_ref[c, :, 0:HEAD_DIM]

    lax.fori_loop(0, n_chunks, f_body, s_f, unroll=RET_UNROLL)

    def b_body(t, s):
        c = n_chunks - 1 - t
        s_ref[c, HEAD_DIM:2 * HEAD_DIM, :] = s.astype(BF16)
        return cd_b * s + u_ref[c, :, HEAD_DIM:2 * HEAD_DIM]

    lax.fori_loop(0, n_chunks, b_body, s_b, unroll=RET_UNROLL)

    def o_body(c, carry):
        rows = chunk_rows(c)
        q = q_ref[0, rows, :]
        qf = q.astype(F32)
        p = _dot(q, kt_ref[0, c])
        lhs = jnp.concatenate(
            [(p * mask).astype(BF16), (qf * xi_f).astype(BF16), (qf * xi_b).astype(BF16)], axis=1)
        rhs = jnp.concatenate([v_ref[0, rows, :], s_ref[c]], axis=0)
        o = _dot(lhs, rhs)
        y = o * lax.rsqrt(jnp.mean(o * o, axis=-1, keepdims=True) + EPS)
        o_ref[0, rows, :] = (y * sg_ref[0, rows, :].astype(F32)).astype(BF16)
        return carry

    lax.fori_loop(0, n_chunks, o_body, 0, unroll=RET_UNROLL)


def _ret_call(rd, ktc, vc, q, kt, v, sg, *, batch, seq, ctx_len):
    n_chunks = seq // CHUNK
    n_ctx_chunks = ctx_len // CHUNK
    x_spec = pl.BlockSpec((1, seq, HEAD_DIM), lambda b, h: (h, b, 0))
    kt_spec = pl.BlockSpec((1, n_chunks, HEAD_DIM, CHUNK), lambda b, h: (h, b, 0, 0))
    vc_spec = pl.BlockSpec((1, ctx_len, HEAD_DIM), lambda b, h: (h, b, 0))
    ktc_spec = pl.BlockSpec((1, n_ctx_chunks, HEAD_DIM, CHUNK), lambda b, h: (h, b, 0, 0))
    return pl.pallas_call(
        functools.partial(_ret_kernel, n_chunks=n_chunks, n_ctx_chunks=n_ctx_chunks),
        grid=(batch, HEADS),
        in_specs=[pl.BlockSpec((2, 1, 8, HEAD_DIM), lambda b, h: (0, h, 0, 0)),
                  ktc_spec, vc_spec, x_spec, kt_spec, x_spec, x_spec],
        out_specs=x_spec,
        out_shape=jax.ShapeDtypeStruct((HEADS, batch * seq, HEAD_DIM), BF16),
        scratch_shapes=[pltpu.VMEM((n_chunks, HEAD_DIM, 2 * HEAD_DIM), F32),
                        pltpu.VMEM((n_chunks, 2 * HEAD_DIM, HEAD_DIM), BF16)],
        compiler_params=pltpu.CompilerParams(
            dimension_semantics=("parallel", "parallel"), vmem_limit_bytes=VMEM_LIMIT),
        name="retention",
    )(rd, ktc, vc, q, kt, v, sg)


def _mix_kernel(ret_ref, gu_ref, vn_ref, ws_ref, bs_ref, wo_ref, o_ref, mix_ref):
    tm = o_ref.shape[0]
    for h in range(HEADS):
        mix_ref[:, h * HEAD_DIM:(h + 1) * HEAD_DIM] = ret_ref[h]
    for c in range(tm // CHUNK):
        rows = slice(c * CHUNK, (c + 1) * CHUNK)
        for gi in range(GROUPS):
            cols = slice(gi * GROUP_DIM, (gi + 1) * GROUP_DIM)
            mixed = _dot(ws_ref[gi], vn_ref[rows, cols]) + bs_ref[gi]
            mix_ref[rows, RET_WIDTH + gi * GROUP_DIM:RET_WIDTH + (gi + 1) * GROUP_DIM] = (
                gu_ref[rows, cols].astype(F32) * mixed).astype(BF16)
    o_ref[...] = _dot(mix_ref[...], wo_ref[...]).astype(BF16)


def _mix_call(ret, gu, vn, ws, bs, w_out):
    rows = gu.shape[0]
    tm = ROW_TILE
    return pl.pallas_call(
        _mix_kernel,
        grid=(rows // tm,),
        in_specs=[
            pl.BlockSpec((HEADS, tm, HEAD_DIM), lambda i: (0, i, 0)),
            pl.BlockSpec((tm, GMLP_WIDTH), lambda i: (i, 0)),
            pl.BlockSpec((tm, GMLP_WIDTH), lambda i: (i, 0)),
            pl.BlockSpec(ws.shape, lambda i: (0, 0, 0)),
            pl.BlockSpec(bs.shape, lambda i: (0, 0, 0)),
            pl.BlockSpec(w_out.shape, lambda i: (0, 0)),
        ],
        out_specs=pl.BlockSpec((tm, D_MODEL), lambda i: (i, 0)),
        out_shape=jax.ShapeDtypeStruct((rows, D_MODEL), BF16),
        scratch_shapes=[pltpu.VMEM((tm, RET_WIDTH + GMLP_WIDTH), BF16)],
        compiler_params=pltpu.CompilerParams(
            dimension_semantics=("parallel",), vmem_limit_bytes=VMEM_LIMIT),
        name="mix",
    )(ret, gu, vn, ws, bs, w_out)


def _rope_tables(seq):
    rows = seq // GRID_W
    row = np.repeat(np.arange(rows, dtype=np.float64), GRID_W)
    col = np.tile(np.arange(GRID_W, dtype=np.float64), rows)
    n_freq = HEAD_DIM // 4
    freqs = ROPE_BASE ** (-np.arange(n_freq, dtype=np.float64) / n_freq)
    ang = np.concatenate([row[:, None] * freqs, col[:, None] * freqs], axis=-1)
    cos, sin = np.cos(ang), np.sin(ang)
    return (jnp.asarray(np.concatenate([cos, cos], axis=-1), F32),
            jnp.asarray(np.concatenate([-sin, sin], axis=-1), F32))


def kernel(x, c, ctx, c_ctx, w_ada, b_ada, norm_g, ffn1_w1, ffn1_w2, w_in, ret_decay,
           gmlp_ws, gmlp_bs, w_out, ffn2_w1, ffn2_w2, final_g):
    batch, seq, d = x.shape
    ctx_len = ctx.shape[1]
    assert w_ada.shape[0] == 1, "single trunk layer only"
    assert d == D_MODEL and seq % ROW_TILE == 0 and (batch * ctx_len) % ROW_TILE == 0
    assert batch + 1 <= MOD_ROWS
    tiles_per_batch = seq // ROW_TILE
    ctx_rows = batch * ctx_len

    xr = x.reshape(batch * seq, d)
    ctxr = ctx.reshape(ctx_rows, d)
    cc = jnp.concatenate([c, c_ctx[None], jnp.zeros((MOD_ROWS - batch - 1, d), F32)], axis=0)
    mods3 = _mods_call(cc, w_ada[0], b_ada[0][None]).reshape(MOD_ROWS, 1, N_MOD * d)

    bf = lambda w: w.astype(BF16)
    g0, g1, g2 = norm_g[0, 0][None], norm_g[0, 1][None], norm_g[0, 2][None]
    x1, w_in_b, w_out_b, w1b, w2b = _ffn1_call(
        xr, ctxr, mods3, g0, ffn1_w1[0], ffn1_w2[0], rows_per_mod=seq, ctx_mod_row=batch,
        cast_weights=(w_in[0], w_out[0], ffn2_w1[0], ffn2_w2[0]))

    q, kt, v, sg, gu, vn = _proj_call(
        x1, mods3, g1, w_in_b, outs=("q", "k", "v", "g", "u", "vg"), row0=0, rows=batch * seq,
        tiles_per_batch=tiles_per_batch, mod_row0=0, rope_tabs=_rope_tables(seq))
    ktc, vc = _proj_call(x1, mods3, g1, w_in_b, outs=("k", "v"), row0=batch * seq, rows=ctx_rows,
                        tiles_per_batch=ctx_rows // ROW_TILE, mod_row0=batch)

    rd = jnp.broadcast_to(ret_decay[0].astype(F32)[:, :, None, None], (2, HEADS, 8, HEAD_DIM))
    ret = _ret_call(rd, ktc, vc, q, kt, v, sg, batch=batch, seq=seq, ctx_len=ctx_len)

    bs = gmlp_bs[0][:, :, None]
    y = _mix_call(ret, gu, vn, bf(gmlp_ws[0]), bs, w_out_b)
    out = _ffn2_call(x1, y, mods3, g2, w1b, w2b, final_g[None], rows=batch * seq, rows_per_mod=seq)
    return out.reshape(batch, seq, d)
```

```python
import functools

import jax
import jax.numpy as jnp
import numpy as np
from jax import lax
from jax.experimental import pallas as pl
from jax.experimental.pallas import tpu as pltpu

F32 = jnp.float32
BF16 = jnp.bfloat16

D_MODEL = 1024
CHUNK = 128
HEADS = 4
HEAD_DIM = 128
GROUPS = 4
GROUP_DIM = 128
RET_WIDTH = HEADS * HEAD_DIM
GMLP_WIDTH = GROUPS * GROUP_DIM
D_FF = 2816
N_MOD = 9
GRID_W = 64
ROPE_BASE = 10000.0
EPS = 1e-6

ROW_TILE = 1024
FF_TILE = 256
BF16_ROWS = 16
MOD_ROWS = 8
RET_UNROLL = 32
VMEM_LIMIT = 56 * 1024 * 1024


def _rmsnorm(x, g):
    return x * lax.rsqrt(jnp.mean(x * x, axis=-1, keepdims=True) + EPS) * g


def _mod_slice(mod_ref, k):
    return mod_ref[0, :, k * D_MODEL:(k + 1) * D_MODEL]


def _dot(a, b):
    return jnp.dot(a, b, preferred_element_type=F32)


def _mods_kernel(c_ref, w_ref, b_ref, o_ref):
    s = jax.nn.silu(c_ref[...]).astype(BF16)
    o_ref[...] = _dot(s, w_ref[...].astype(BF16)) + b_ref[...]


def _mods_call(cc, w_ada, b_ada):
    n_out = w_ada.shape[1]
    tn = D_MODEL
    return pl.pallas_call(
        _mods_kernel,
        grid=(n_out // tn,),
        in_specs=[
            pl.BlockSpec((MOD_ROWS, D_MODEL), lambda j: (0, 0)),
            pl.BlockSpec((D_MODEL, tn), lambda j: (0, j)),
            pl.BlockSpec((1, tn), lambda j: (0, j)),
        ],
        out_specs=pl.BlockSpec((MOD_ROWS, tn), lambda j: (0, j)),
        out_shape=jax.ShapeDtypeStruct((MOD_ROWS, n_out), F32),
        compiler_params=pltpu.CompilerParams(
            dimension_semantics=("arbitrary",), vmem_limit_bytes=VMEM_LIMIT),
        name="mods",
    )(cc, w_ada, b_ada)


N_FF_CHUNKS = D_FF // FF_TILE
FFN_ROW_TILE = 512
W_LOAD_CHUNKS = 16
W_LOAD_SLOTS = 4


def _swiglu_chunks(xn_ref, slot, w1_ref, w2_ref, between):
    acc = None
    for s in range(N_FF_CHUNKS):
        cols = slice(s * FF_TILE, (s + 1) * FF_TILE)
        gate = _dot(xn_ref[slot], w1_ref[:, cols])
        up = _dot(xn_ref[slot], w1_ref[:, D_FF + s * FF_TILE:D_FF + (s + 1) * FF_TILE])
        h = (jax.nn.silu(gate) * up).astype(BF16)
        d = _dot(h, w2_ref[cols, :])
        acc = d if acc is None else acc + d
        between(s)
    return acc


def _piece(tm, s):
    n = -(-tm // (N_FF_CHUNKS * BF16_ROWS)) * BF16_ROWS
    return min(s * n, tm - n), n


def _ffn1_kernel(x_ref, xnext_ref, c_ref, mod_ref, modn_ref, g_ref, w1_hbm, w2_hbm, *rest,
                 n_cast, n_x_tiles, n_c_tiles):
    rest = list(rest)
    cast_in = [rest.pop(0) for _ in range(n_cast)]
    o_ref = rest.pop(0)
    cast_out = [rest.pop(0) for _ in range(n_cast)]
    xn_ref, w1_ref, w2_ref, st1_ref, st2_ref, sem1, sem2 = rest
    i = pl.program_id(0)
    slot = i % 2
    tm = o_ref.shape[0]

    def tile_rows(j, lat_ref, r0, n):
        c0 = jnp.clip(j - n_x_tiles, 0, n_c_tiles - 1) * tm + r0
        ctx_rows = c_ref[pl.ds(pl.multiple_of(c0, BF16_ROWS), n), :]
        return jnp.where(j < n_x_tiles, lat_ref[r0:r0 + n, :], ctx_rows)

    def normalized(x, m_ref):
        y = _rmsnorm(x, g_ref[...])
        return (y * (1 + _mod_slice(m_ref, 1)) + _mod_slice(m_ref, 0)).astype(BF16)

    def load_weight(hbm, dst, stage, sem):
        rows = dst.shape[0] // W_LOAD_CHUNKS

        def copy(c):
            s = c % W_LOAD_SLOTS
            return pltpu.make_async_copy(hbm.at[pl.ds(c * rows, rows), :], stage.at[s], sem.at[s])

        for c in range(W_LOAD_SLOTS - 1):
            copy(c).start()
        for c in range(W_LOAD_CHUNKS):
            if c + W_LOAD_SLOTS - 1 < W_LOAD_CHUNKS:
                copy(c + W_LOAD_SLOTS - 1).start()
            copy(c).wait()
            dst[c * rows:(c + 1) * rows, :] = stage[c % W_LOAD_SLOTS].astype(BF16)

    @pl.when(i == 0)
    def _():
        load_weight(w1_hbm, w1_ref, st1_ref, sem1)
        load_weight(w2_hbm, w2_ref, st2_ref, sem2)
        xn_ref[0] = normalized(x_ref[...], mod_ref)

    def between(s):
        r0, n = _piece(tm, s)
        xn_ref[1 - slot, r0:r0 + n, :] = normalized(tile_rows(i + 1, xnext_ref, r0, n), modn_ref)
        if s < n_cast:
            cast_out[s][...] = cast_in[s][...].astype(BF16)

    acc = _swiglu_chunks(xn_ref, slot, w1_ref, w2_ref, between)
    o_ref[...] = tile_rows(i, x_ref, 0, tm) + 0.5 * _mod_slice(mod_ref, 2) * acc


def _ffn1_call(xr, ctxr, mods3, g, w1, w2, *, rows_per_mod, ctx_mod_row, cast_weights):
    tm = FFN_ROW_TILE
    n_x, n_c = xr.shape[0] // tm, ctxr.shape[0] // tm
    assert n_x * tm == xr.shape[0] and n_c * tm == ctxr.shape[0] and rows_per_mod % tm == 0
    n_tiles = n_x + n_c
    tpb = rows_per_mod // tm
    lat = lambda j: jnp.minimum(j, n_x - 1)
    nxt = lambda i: jnp.minimum(i + 1, n_tiles - 1)
    mod_row = lambda j: jnp.where(j < n_x, j // tpb, ctx_mod_row)
    cast_specs, cast_shapes = [], []
    for w in cast_weights:
        per = 1 if (w.shape[0] // n_x) % BF16_ROWS == 0 else 2
        slab = w.shape[0] * per // n_x
        assert slab * n_x == w.shape[0] * per and slab % BF16_ROWS == 0, (w.shape, n_x)
        cast_specs.append(pl.BlockSpec((slab, w.shape[1]), lambda i, per=per: (lat(i) // per, 0)))
        cast_shapes.append(jax.ShapeDtypeStruct(w.shape, BF16))
    st1_rows, st2_rows = w1.shape[0] // W_LOAD_CHUNKS, w2.shape[0] // W_LOAD_CHUNKS
    assert st1_rows % BF16_ROWS == 0 and st2_rows % BF16_ROWS == 0
    in_specs = [
        pl.BlockSpec((tm, D_MODEL), lambda i: (lat(i), 0)),
        pl.BlockSpec((tm, D_MODEL), lambda i: (lat(i + 1), 0)),
        pl.BlockSpec(ctxr.shape, lambda i: (0, 0), pipeline_mode=pl.Buffered(1)),
        pl.BlockSpec((1, 1, N_MOD * D_MODEL), lambda i: (mod_row(i), 0, 0)),
        pl.BlockSpec((1, 1, N_MOD * D_MODEL), lambda i: (mod_row(nxt(i)), 0, 0)),
        pl.BlockSpec((1, D_MODEL), lambda i: (0, 0)),
        pl.BlockSpec(memory_space=pl.ANY),
        pl.BlockSpec(memory_space=pl.ANY),
    ]
    return pl.pallas_call(
        functools.partial(_ffn1_kernel, n_cast=len(cast_weights), n_x_tiles=n_x, n_c_tiles=n_c),
        grid=(n_tiles,),
        in_specs=in_specs + cast_specs,
        out_specs=[pl.BlockSpec((tm, D_MODEL), lambda i: (i, 0))] + cast_specs,
        out_shape=[jax.ShapeDtypeStruct((n_tiles * tm, D_MODEL), F32)] + cast_shapes,
        scratch_shapes=[pltpu.VMEM((2, tm, D_MODEL), BF16),
                        pltpu.VMEM(w1.shape, BF16), pltpu.VMEM(w2.shape, BF16),
                        pltpu.VMEM((W_LOAD_SLOTS, st1_rows, w1.shape[1]), F32),
                        pltpu.VMEM((W_LOAD_SLOTS, st2_rows, w2.shape[1]), F32),
                        pltpu.SemaphoreType.DMA((W_LOAD_SLOTS,)),
                        pltpu.SemaphoreType.DMA((W_LOAD_SLOTS,))],
        compiler_params=pltpu.CompilerParams(
            dimension_semantics=("arbitrary",), vmem_limit_bytes=VMEM_LIMIT),
        name="ffn1",
    )(xr, xr, ctxr, mods3, mods3, g, w1, w2, *cast_weights)


def _ffn2_kernel(x_ref, xnext_ref, y_ref, ynext_ref, mod_ref, modn_ref, g_ref, w1_ref, w2_ref, fg_ref,
                 o_ref, xn_ref):
    i = pl.program_id(0)
    slot = i % 2
    tm = x_ref.shape[0]

    def x2(xr, yr, m_ref, rows=slice(None)):
        return xr[rows, :] + _mod_slice(m_ref, 5) * yr[rows, :].astype(F32)

    def normalized(x, m_ref):
        y = _rmsnorm(x, g_ref[...])
        return (y * (1 + _mod_slice(m_ref, 7)) + _mod_slice(m_ref, 6)).astype(BF16)

    @pl.when(i == 0)
    def _():
        xn_ref[0] = normalized(x2(x_ref, y_ref, mod_ref), mod_ref)

    def between(s):
        r0, n = _piece(tm, s)
        xn_ref[1 - slot, r0:r0 + n, :] = normalized(
            x2(xnext_ref, ynext_ref, modn_ref, slice(r0, r0 + n)), modn_ref)

    acc = _swiglu_chunks(xn_ref, slot, w1_ref, w2_ref, between)
    out = x2(x_ref, y_ref, mod_ref) + 0.5 * _mod_slice(mod_ref, 8) * acc
    o_ref[...] = _rmsnorm(out, fg_ref[...])


def _ffn2_call(xr, y, mods3, g, w1, w2, final_g, *, rows, rows_per_mod):
    tm = FFN_ROW_TILE
    n_tiles = rows // tm
    assert n_tiles * tm == rows and rows_per_mod % tm == 0
    tpb = rows_per_mod // tm
    nxt = lambda i: jnp.minimum(i + 1, n_tiles - 1)
    cur_spec = pl.BlockSpec((tm, D_MODEL), lambda i: (i, 0))
    nxt_spec = pl.BlockSpec((tm, D_MODEL), lambda i: (nxt(i), 0))
    vec_spec = pl.BlockSpec((1, D_MODEL), lambda i: (0, 0))
    resident = dict(pipeline_mode=pl.Buffered(1))
    return pl.pallas_call(
        _ffn2_kernel,
        grid=(n_tiles,),
        in_specs=[
            cur_spec, nxt_spec, cur_spec, nxt_spec,
            pl.BlockSpec((1, 1, N_MOD * D_MODEL), lambda i: (i // tpb, 0, 0)),
            pl.BlockSpec((1, 1, N_MOD * D_MODEL), lambda i: (nxt(i) // tpb, 0, 0)),
            vec_spec,
            pl.BlockSpec(w1.shape, lambda i: (0, 0), **resident),
            pl.BlockSpec(w2.shape, lambda i: (0, 0), **resident),
            vec_spec,
        ],
        out_specs=cur_spec,
        out_shape=jax.ShapeDtypeStruct((rows, D_MODEL), F32),
        scratch_shapes=[pltpu.VMEM((2, tm, D_MODEL), BF16)],
        compiler_params=pltpu.CompilerParams(
            dimension_semantics=("arbitrary",), vmem_limit_bytes=VMEM_LIMIT),
        name="ffn2",
    )(xr, xr, y, y, mods3, mods3, g, w1, w2, final_g)


_PROJ_COL = {"q": 0, "k": 1, "v": 2, "g": 3, "u": 4, "vg": 5}
_PROJ_ORDER = ("vg", "k", "q", "u", "g", "v")


def _proj_kernel(x_ref, mod_ref, g_ref, w_ref, *rest, outs, rope):
    if rope:
        cs_ref, sn_ref = rest[:2]
        rest = rest[2:]
    o_refs = dict(zip(outs, rest))
    y = _rmsnorm(x_ref[...], g_ref[...])
    xn = (y * (1 + _mod_slice(mod_ref, 4)) + _mod_slice(mod_ref, 3)).astype(BF16)
    k_scale = HEAD_DIM ** -0.5
    for name in sorted(outs, key=_PROJ_ORDER.index):
        col = _PROJ_COL[name] * RET_WIDTH
        p = _dot(xn, w_ref[:, col:col + RET_WIDTH])
        o_ref = o_refs[name]
        if name in ("q", "k", "v", "g"):
            for h in range(HEADS):
                t = p[:, h * HEAD_DIM:(h + 1) * HEAD_DIM]
                if name in ("q", "k") and rope:
                    t = t * cs_ref[...] + pltpu.roll(t, HEAD_DIM // 2, 1) * sn_ref[...]
                if name == "k":
                    t = t * k_scale
                if name == "g":
                    t = jax.nn.silu(t)
                if name == "k":
                    for cc in range(t.shape[0] // CHUNK):
                        o_ref[h, cc] = t[cc * CHUNK:(cc + 1) * CHUNK, :].T.astype(BF16)
                else:
                    o_ref[h] = t.astype(BF16)
        elif name == "u":
            o_ref[...] = jax.nn.gelu(p).astype(BF16)
        else:
            gv = jax.nn.gelu(p)
            for gi in range(GROUPS):
                t = gv[:, gi * GROUP_DIM:(gi + 1) * GROUP_DIM]
                mu = jnp.mean(t, axis=-1, keepdims=True)
                d = t - mu
                var = jnp.mean(d * d, axis=-1, keepdims=True)
                o_ref[:, gi * GROUP_DIM:(gi + 1) * GROUP_DIM] = (d * lax.rsqrt(var + EPS)).astype(BF16)


def _proj_call(xr, mods3, g, w_in, *, outs, row0, rows, tiles_per_batch, mod_row0, rope_tabs=None):
    tm = ROW_TILE
    assert row0 % tm == 0 and rows % tm == 0
    tile0 = row0 // tm
    mod_map = lambda i: (mod_row0 + i // tiles_per_batch, 0, 0)
    in_specs = [
        pl.BlockSpec((tm, D_MODEL), lambda i: (tile0 + i, 0)),
        pl.BlockSpec((1, 1, N_MOD * D_MODEL), mod_map),
        pl.BlockSpec((1, D_MODEL), lambda i: (0, 0)),
        pl.BlockSpec(w_in.shape, lambda i: (0, 0)),
    ]
    args = [xr, mods3, g, w_in]
    if rope_tabs is not None:
        tab_map = lambda i: (i % tiles_per_batch, 0)
        in_specs += [pl.BlockSpec((tm, HEAD_DIM), tab_map)] * 2
        args += list(rope_tabs)
    out_specs, out_shape = [], []
    for name in outs:
        if name == "k":
            out_specs.append(pl.BlockSpec((HEADS, tm // CHUNK, HEAD_DIM, CHUNK), lambda i: (0, i, 0, 0)))
            out_shape.append(jax.ShapeDtypeStruct((HEADS, rows // CHUNK, HEAD_DIM, CHUNK), BF16))
        elif name in ("q", "v", "g"):
            out_specs.append(pl.BlockSpec((HEADS, tm, HEAD_DIM), lambda i: (0, i, 0)))
            out_shape.append(jax.ShapeDtypeStruct((HEADS, rows, HEAD_DIM), BF16))
        else:
            out_specs.append(pl.BlockSpec((tm, GMLP_WIDTH), lambda i: (i, 0)))
            out_shape.append(jax.ShapeDtypeStruct((rows, GMLP_WIDTH), BF16))
    return pl.pallas_call(
        functools.partial(_proj_kernel, outs=outs, rope=rope_tabs is not None),
        grid=(rows // tm,),
        in_specs=in_specs,
        out_specs=out_specs,
        out_shape=out_shape,
        compiler_params=pltpu.CompilerParams(
            dimension_semantics=("parallel",), vmem_limit_bytes=VMEM_LIMIT),
        name="proj_x" if rope_tabs is not None else "proj_ctx",
    )(*args)


def _ret_kernel(rd_ref, ktc_ref, vc_ref, q_ref, kt_ref, v_ref, sg_ref, o_ref, u_ref, s_ref,
                *, n_chunks, n_ctx_chunks):
    L = CHUNK
    lg_f = -jnp.exp(rd_ref[0, 0, 0:1, :])
    lg_b = -jnp.exp(rd_ref[1, 0, 0:1, :])
    ii = lax.broadcasted_iota(jnp.int32, (L, L), 0).astype(F32)
    jj = lax.broadcasted_iota(jnp.int32, (L, L), 1).astype(F32)
    diff = ii - jj
    lower = diff >= 0
    upper = diff <= 0
    mask = (jnp.where(lower, jnp.exp(jnp.where(lower, diff, 0.0) * lg_f), 0.0)
            + jnp.where(upper, jnp.exp(jnp.where(upper, -diff, 0.0) * lg_b), 0.0))
    xi_f = jnp.exp((ii + 1.0) * lg_f)
    ze_f = jnp.exp((L - 1.0 - ii) * lg_f)
    xi_b = jnp.exp((L - ii) * lg_b)
    ze_b = jnp.exp(ii * lg_b)
    cd_f = jnp.exp(L * lg_f)
    cd_b = jnp.exp(L * lg_b)

    def kv_pair(kt, v):
        vf = v.astype(F32)
        vz = jnp.concatenate([(vf * ze_f).astype(BF16), (vf * ze_b).astype(BF16)], axis=1)
        return _dot(kt, vz)

    def chunk_rows(c):
        return pl.ds(pl.multiple_of(c * L, L), L)

    uc = [kv_pair(ktc_ref[0, c], vc_ref[0, c * L:(c + 1) * L, :]) for c in range(n_ctx_chunks)]
    s_f = jnp.zeros((HEAD_DIM, HEAD_DIM), F32)
    for c in range(n_ctx_chunks):
        s_f = cd_f * s_f + uc[c][:, :HEAD_DIM]
    s_b = jnp.zeros((HEAD_DIM, HEAD_DIM), F32)
    for c in reversed(range(n_ctx_chunks)):
        s_b = cd_b * s_b + uc[c][:, HEAD_DIM:]

    def u_body(c, carry):
        u_ref[c] = kv_pair(kt_ref[0, c], v_ref[0, chunk_rows(c), :])
        return carry

    lax.fori_loop(0, n_chunks, u_body, 0, unroll=RET_UNROLL)

    def f_body(c, s):
        s_ref[c, 0:HEAD_DIM, :] = s.astype(BF16)
        return cd_f * s + u_ref[c, :, 0:HEAD_DIM]

    lax.fori_loop(0, n_chunks, f_body, s_f, unroll=RET_UNROLL)

    def b_body(t, s):
        c = n_chunks - 1 - t
        s_ref[c, HEAD_DIM:2 * HEAD_DIM, :] = s.astype(BF16)
        return cd_b * s + u_ref[c, :, HEAD_DIM:2 * HEAD_DIM]

    lax.fori_loop(0, n_chunks, b_body, s_b, unroll=RET_UNROLL)

    def o_body(c, carry):
        rows = chunk_rows(c)
        q = q_ref[0, rows, :]
        qf = q.astype(F32)
        p = _dot(q, kt_ref[0, c])
        lhs = jnp.concatenate(
            [(p * mask).astype(BF16), (qf * xi_f).astype(BF16), (qf * xi_b).astype(BF16)], axis=1)
        rhs = jnp.concatenate([v_ref[0, rows, :], s_ref[c]], axis=0)
        o = _dot(lhs, rhs)
        y = o * lax.rsqrt(jnp.mean(o * o, axis=-1, keepdims=True) + EPS)
        o_ref[0, rows, :] = (y * sg_ref[0, rows, :].astype(F32)).astype(BF16)
        return carry

    lax.fori_loop(0, n_chunks, o_body, 0, unroll=RET_UNROLL)


def _ret_call(rd, ktc, vc, q, kt, v, sg, *, batch, seq, ctx_len):
    n_chunks = seq // CHUNK
    n_ctx_chunks = ctx_len // CHUNK
    x_spec = pl.BlockSpec((1, seq, HEAD_DIM), lambda b, h: (h, b, 0))
    kt_spec = pl.BlockSpec((1, n_chunks, HEAD_DIM, CHUNK), lambda b, h: (h, b, 0, 0))
    vc_spec = pl.BlockSpec((1, ctx_len, HEAD_DIM), lambda b, h: (h, b, 0))
    ktc_spec = pl.BlockSpec((1, n_ctx_chunks, HEAD_DIM, CHUNK), lambda b, h: (h, b, 0, 0))
    return pl.pallas_call(
        functools.partial(_ret_kernel, n_chunks=n_chunks, n_ctx_chunks=n_ctx_chunks),
        grid=(batch, HEADS),
        in_specs=[pl.BlockSpec((2, 1, 8, HEAD_DIM), lambda b, h: (0, h, 0, 0)),
                  ktc_spec, vc_spec, x_spec, kt_spec, x_spec, x_spec],
        out_specs=x_spec,
        out_shape=jax.ShapeDtypeStruct((HEADS, batch * seq, HEAD_DIM), BF16),
        scratch_shapes=[pltpu.VMEM((n_chunks, HEAD_DIM, 2 * HEAD_DIM), F32),
                        pltpu.VMEM((n_chunks, 2 * HEAD_DIM, HEAD_DIM), BF16)],
        compiler_params=pltpu.CompilerParams(
            dimension_semantics=("parallel", "parallel"), vmem_limit_bytes=VMEM_LIMIT),
        name="retention",
    )(rd, ktc, vc, q, kt, v, sg)


def _mix_kernel(ret_ref, gu_ref, vn_ref, ws_ref, bs_ref, wo_ref, o_ref, mix_ref):
    tm = o_ref.shape[0]
    for h in range(HEADS):
        mix_ref[:, h * HEAD_DIM:(h + 1) * HEAD_DIM] = ret_ref[h]
    for c in range(tm // CHUNK):
        rows = slice(c * CHUNK, (c + 1) * CHUNK)
        for gi in range(GROUPS):
            cols = slice(gi * GROUP_DIM, (gi + 1) * GROUP_DIM)
            mixed = _dot(ws_ref[gi], vn_ref[rows, cols]) + bs_ref[gi]
            mix_ref[rows, RET_WIDTH + gi * GROUP_DIM:RET_WIDTH + (gi + 1) * GROUP_DIM] = (
                gu_ref[rows, cols].astype(F32) * mixed).astype(BF16)
    o_ref[...] = _dot(mix_ref[...], wo_ref[...]).astype(BF16)


def _mix_call(ret, gu, vn, ws, bs, w_out):
    rows = gu.shape[0]
    tm = ROW_TILE
    return pl.pallas_call(
        _mix_kernel,
        grid=(rows // tm,),
        in_specs=[
            pl.BlockSpec((HEADS, tm, HEAD_DIM), lambda i: (0, i, 0)),
            pl.BlockSpec((tm, GMLP_WIDTH), lambda i: (i, 0)),
            pl.BlockSpec((tm, GMLP_WIDTH), lambda i: (i, 0)),
            pl.BlockSpec(ws.shape, lambda i: (0, 0, 0)),
            pl.BlockSpec(bs.shape, lambda i: (0, 0, 0)),
            pl.BlockSpec(w_out.shape, lambda i: (0, 0)),
        ],
        out_specs=pl.BlockSpec((tm, D_MODEL), lambda i: (i, 0)),
        out_shape=jax.ShapeDtypeStruct((rows, D_MODEL), BF16),
        scratch_shapes=[pltpu.VMEM((tm, RET_WIDTH + GMLP_WIDTH), BF16)],
        compiler_params=pltpu.CompilerParams(
            dimension_semantics=("parallel",), vmem_limit_bytes=VMEM_LIMIT),
        name="mix",
    )(ret, gu, vn, ws, bs, w_out)


def _rope_tables(seq):
    rows = seq // GRID_W
    row = np.repeat(np.arange(rows, dtype=np.float64), GRID_W)
    col = np.tile(np.arange(GRID_W, dtype=np.float64), rows)
    n_freq = HEAD_DIM // 4
    freqs = ROPE_BASE ** (-np.arange(n_freq, dtype=np.float64) / n_freq)
    ang = np.concatenate([row[:, None] * freqs, col[:, None] * freqs], axis=-1)
    cos, sin = np.cos(ang), np.sin(ang)
    return (jnp.asarray(np.concatenate([cos, cos], axis=-1), F32),
            jnp.asarray(np.concatenate([-sin, sin], axis=-1), F32))


def kernel(x, c, ctx, c_ctx, w_ada, b_ada, norm_g, ffn1_w1, ffn1_w2, w_in, ret_decay,
           gmlp_ws, gmlp_bs, w_out, ffn2_w1, ffn2_w2, final_g):
    batch, seq, d = x.shape
    ctx_len = ctx.shape[1]
    assert w_ada.shape[0] == 1, "single trunk layer only"
    assert d == D_MODEL and seq % ROW_TILE == 0 and (batch * ctx_len) % ROW_TILE == 0
    assert batch + 1 <= MOD_ROWS
    tiles_per_batch = seq // ROW_TILE
    ctx_rows = batch * ctx_len

    xr = x.reshape(batch * seq, d)
    ctxr = ctx.reshape(ctx_rows, d)
    cc = jnp.concatenate([c, c_ctx[None], jnp.zeros((MOD_ROWS - batch - 1, d), F32)], axis=0)
    mods3 = _mods_call(cc, w_ada[0], b_ada[0][None]).reshape(MOD_ROWS, 1, N_MOD * d)

    bf = lambda w: w.astype(BF16)
    g0, g1, g2 = norm_g[0, 0][None], norm_g[0, 1][None], norm_g[0, 2][None]
    x1, w_in_b, w_out_b, w1b, w2b = _ffn1_call(
        xr, ctxr, mods3, g0, ffn1_w1[0], ffn1_w2[0], rows_per_mod=seq, ctx_mod_row=batch,
        cast_weights=(w_in[0], w_out[0], ffn2_w1[0], ffn2_w2[0]))

    q, kt, v, sg, gu, vn = _proj_call(
        x1, mods3, g1, w_in_b, outs=("q", "k", "v", "g", "u", "vg"), row0=0, rows=batch * seq,
        tiles_per_batch=tiles_per_batch, mod_row0=0, rope_tabs=_rope_tables(seq))
    ktc, vc = _proj_call(x1, mods3, g1, w_in_b, outs=("k", "v"), row0=batch * seq, rows=ctx_rows,
                        tiles_per_batch=ctx_rows // ROW_TILE, mod_row0=batch)

    rd = jnp.broadcast_to(ret_decay[0].astype(F32)[:, :, None, None], (2, HEADS, 8, HEAD_DIM))
    ret = _ret_call(rd, ktc, vc, q, kt, v, sg, batch=batch, seq=seq, ctx_len=ctx_len)

    bs = gmlp_bs[0][:, :, None]
    y = _mix_call(ret, gu, vn, bf(gmlp_ws[0]), bs, w_out_b)
    out = _ffn2_call(x1, y, mods3, g2, w1b, w2b, final_g[None], rows=batch * seq, rows_per_mod=seq)
    return out.reshape(batch, seq, d)
```

```python
import functools

import jax
import jax.numpy as jnp
import numpy as np
from jax import lax
from jax.experimental import pallas as pl
from jax.experimental.pallas import tpu as pltpu

F32 = jnp.float32
BF16 = jnp.bfloat16

D_MODEL = 1024
CHUNK = 128
HEADS = 4
HEAD_DIM = 128
GROUPS = 4
GROUP_DIM = 128
RET_WIDTH = HEADS * HEAD_DIM
GMLP_WIDTH = GROUPS * GROUP_DIM
D_FF = 2816
N_MOD = 9
GRID_W = 64
ROPE_BASE = 10000.0
EPS = 1e-6

ROW_TILE = 1024
FF_TILE = 256
BF16_ROWS = 16
MOD_ROWS = 8
RET_UNROLL = 32
VMEM_LIMIT = 56 * 1024 * 1024


def _rmsnorm(x, g):
    return x * lax.rsqrt(jnp.mean(x * x, axis=-1, keepdims=True) + EPS) * g


def _mod_slice(mod_ref, k):
    return mod_ref[0, :, k * D_MODEL:(k + 1) * D_MODEL]


def _dot(a, b):
    return jnp.dot(a, b, preferred_element_type=F32)


def _mods_kernel(c_ref, cctx_ref, w_ref, b_ref, o_ref, s_ref):
    batch = c_ref.shape[0]

    @pl.when(pl.program_id(0) == 0)
    def _():
        s_ref[...] = jnp.zeros_like(s_ref)
        s_ref[0:batch, :] = jax.nn.silu(c_ref[...])
        s_ref[batch:batch + 1, :] = jax.nn.silu(cctx_ref[...])

    res = _dot(s_ref[...].astype(BF16), w_ref[...].astype(BF16)) + b_ref[...]
    o_ref[:, 0, :] = res


def _mods_call(c, c_ctx, w_ada, b_ada):
    n_out = w_ada.shape[1]
    tn = D_MODEL
    return pl.pallas_call(
        _mods_kernel,
        grid=(n_out // tn,),
        in_specs=[
            pl.BlockSpec(c.shape, lambda j: (0, 0)),
            pl.BlockSpec(c_ctx.shape, lambda j: (0, 0)),
            pl.BlockSpec((D_MODEL, tn), lambda j: (0, j)),
            pl.BlockSpec((1, tn), lambda j: (0, j)),
        ],
        out_specs=pl.BlockSpec((MOD_ROWS, 1, tn), lambda j: (0, 0, j)),
        out_shape=jax.ShapeDtypeStruct((MOD_ROWS, 1, n_out), F32),
        scratch_shapes=[pltpu.VMEM((MOD_ROWS, D_MODEL), F32)],
        compiler_params=pltpu.CompilerParams(
            dimension_semantics=("arbitrary",), vmem_limit_bytes=VMEM_LIMIT),
        name="mods",
    )(c, c_ctx, w_ada, b_ada)


N_FF_CHUNKS = D_FF // FF_TILE
FFN_ROW_TILE = 512
W_LOAD_CHUNKS = 16
W_LOAD_SLOTS = 4


def _swiglu_chunks(xn_ref, slot, w1_ref, w2_ref, between):
    acc = None
    for s in range(N_FF_CHUNKS):
        cols = slice(s * FF_TILE, (s + 1) * FF_TILE)
        gate = _dot(xn_ref[slot], w1_ref[:, cols])
        up = _dot(xn_ref[slot], w1_ref[:, D_FF + s * FF_TILE:D_FF + (s + 1) * FF_TILE])
        h = (jax.nn.silu(gate) * up).astype(BF16)
        d = _dot(h, w2_ref[cols, :])
        acc = d if acc is None else acc + d
        between(s)
    return acc


def _piece(tm, s):
    n = -(-tm // (N_FF_CHUNKS * BF16_ROWS)) * BF16_ROWS
    return min(s * n, tm - n), n


def _ffn1_kernel(x_ref, xnext_ref, c_ref, mod_ref, modn_ref, g_ref, w1_hbm, w2_hbm, *rest,
                 n_cast, n_x_tiles, n_c_tiles):
    rest = list(rest)
    cast_in = [rest.pop(0) for _ in range(n_cast)]
    o_ref = rest.pop(0)
    cast_out = [rest.pop(0) for _ in range(n_cast)]
    xn_ref, w1_ref, w2_ref, st1_ref, st2_ref, sem1, sem2 = rest
    i = pl.program_id(0)
    slot = i % 2
    tm = o_ref.shape[0]

    def tile_rows(j, lat_ref, r0, n):
        c0 = jnp.clip(j - n_x_tiles, 0, n_c_tiles - 1) * tm + r0
        ctx_rows = c_ref[pl.ds(pl.multiple_of(c0, BF16_ROWS), n), :]
        return jnp.where(j < n_x_tiles, lat_ref[r0:r0 + n, :], ctx_rows)

    def normalized(x, m_ref):
        y = _rmsnorm(x, g_ref[...])
        return (y * (1 + _mod_slice(m_ref, 1)) + _mod_slice(m_ref, 0)).astype(BF16)

    def load_weight(hbm, dst, stage, sem):
        rows = dst.shape[0] // W_LOAD_CHUNKS

        def copy(c):
            s = c % W_LOAD_SLOTS
            return pltpu.make_async_copy(hbm.at[pl.ds(c * rows, rows), :], stage.at[s], sem.at[s])

        for c in range(W_LOAD_SLOTS - 1):
            copy(c).start()
        for c in range(W_LOAD_CHUNKS):
            if c + W_LOAD_SLOTS - 1 < W_LOAD_CHUNKS:
                copy(c + W_LOAD_SLOTS - 1).start()
            copy(c).wait()
            dst[c * rows:(c + 1) * rows, :] = stage[c % W_LOAD_SLOTS].astype(BF16)

    @pl.when(i == 0)
    def _():
        load_weight(w1_hbm, w1_ref, st1_ref, sem1)
        load_weight(w2_hbm, w2_ref, st2_ref, sem2)
        xn_ref[0] = normalized(x_ref[...], mod_ref)

    def between(s):
        r0, n = _piece(tm, s)
        xn_ref[1 - slot, r0:r0 + n, :] = normalized(tile_rows(i + 1, xnext_ref, r0, n), modn_ref)
        if s < n_cast:
            cast_out[s][...] = cast_in[s][...].astype(BF16)

    acc = _swiglu_chunks(xn_ref, slot, w1_ref, w2_ref, between)
    o_ref[...] = tile_rows(i, x_ref, 0, tm) + 0.5 * _mod_slice(mod_ref, 2) * acc


def _ffn1_call(xr, ctxr, mods3, g, w1, w2, *, rows_per_mod, ctx_mod_row, cast_weights):
    tm = FFN_ROW_TILE
    n_x, n_c = xr.shape[0] // tm, ctxr.shape[0] // tm
    assert n_x * tm == xr.shape[0] and n_c * tm == ctxr.shape[0] and rows_per_mod % tm == 0
    n_tiles = n_x + n_c
    tpb = rows_per_mod // tm
    lat = lambda j: jnp.minimum(j, n_x - 1)
    nxt = lambda i: jnp.minimum(i + 1, n_tiles - 1)
    mod_row = lambda j: jnp.where(j < n_x, j // tpb, ctx_mod_row)
    cast_specs, cast_shapes = [], []
    for w in cast_weights:
        per = 1 if (w.shape[0] // n_x) % BF16_ROWS == 0 else 2
        slab = w.shape[0] * per // n_x
        assert slab * n_x == w.shape[0] * per and slab % BF16_ROWS == 0, (w.shape, n_x)
        cast_specs.append(pl.BlockSpec((slab, w.shape[1]), lambda i, per=per: (lat(i) // per, 0)))
        cast_shapes.append(jax.ShapeDtypeStruct(w.shape, BF16))
    st1_rows, st2_rows = w1.shape[0] // W_LOAD_CHUNKS, w2.shape[0] // W_LOAD_CHUNKS
    assert st1_rows % BF16_ROWS == 0 and st2_rows % BF16_ROWS == 0
    in_specs = [
        pl.BlockSpec((tm, D_MODEL), lambda i: (lat(i), 0)),
        pl.BlockSpec((tm, D_MODEL), lambda i: (lat(i + 1), 0)),
        pl.BlockSpec(ctxr.shape, lambda i: (0, 0), pipeline_mode=pl.Buffered(1)),
        pl.BlockSpec((1, 1, N_MOD * D_MODEL), lambda i: (mod_row(i), 0, 0)),
        pl.BlockSpec((1, 1, N_MOD * D_MODEL), lambda i: (mod_row(nxt(i)), 0, 0)),
        pl.BlockSpec((1, D_MODEL), lambda i: (0, 0)),
        pl.BlockSpec(memory_space=pl.ANY),
        pl.BlockSpec(memory_space=pl.ANY),
    ]
    return pl.pallas_call(
        functools.partial(_ffn1_kernel, n_cast=len(cast_weights), n_x_tiles=n_x, n_c_tiles=n_c),
        grid=(n_tiles,),
        in_specs=in_specs + cast_specs,
        out_specs=[pl.BlockSpec((tm, D_MODEL), lambda i: (i, 0))] + cast_specs,
        out_shape=[jax.ShapeDtypeStruct((n_tiles * tm, D_MODEL), F32)] + cast_shapes,
        scratch_shapes=[pltpu.VMEM((2, tm, D_MODEL), BF16),
                        pltpu.VMEM(w1.shape, BF16), pltpu.VMEM(w2.shape, BF16),
                        pltpu.VMEM((W_LOAD_SLOTS, st1_rows, w1.shape[1]), F32),
                        pltpu.VMEM((W_LOAD_SLOTS, st2_rows, w2.shape[1]), F32),
                        pltpu.SemaphoreType.DMA((W_LOAD_SLOTS,)),
                        pltpu.SemaphoreType.DMA((W_LOAD_SLOTS,))],
        compiler_params=pltpu.CompilerParams(
            dimension_semantics=("arbitrary",), vmem_limit_bytes=VMEM_LIMIT),
        name="ffn1",
    )(xr, xr, ctxr, mods3, mods3, g, w1, w2, *cast_weights)


def _ffn2_kernel(x_ref, xnext_ref, y_ref, ynext_ref, mod_ref, modn_ref, g_ref, w1_ref, w2_ref, fg_ref,
                 o_ref, xn_ref, x2_ref):
    i = pl.program_id(0)
    slot = i % 2
    tm = o_ref.shape[0]

    def x2(xr, yr, m_ref, rows=slice(None)):
        return xr[rows, :] + _mod_slice(m_ref, 5) * yr[rows, :].astype(F32)

    def normalized(x, m_ref):
        y = _rmsnorm(x, g_ref[...])
        return (y * (1 + _mod_slice(m_ref, 7)) + _mod_slice(m_ref, 6)).astype(BF16)

    @pl.when(i == 0)
    def _():
        t = x2(x_ref, y_ref, mod_ref)
        x2_ref[0] = t
        xn_ref[0] = normalized(t, mod_ref)

    def between(s):
        r0, n = _piece(tm, s)
        t = x2(xnext_ref, ynext_ref, modn_ref, slice(r0, r0 + n))
        x2_ref[1 - slot, r0:r0 + n, :] = t
        xn_ref[1 - slot, r0:r0 + n, :] = normalized(t, modn_ref)

    acc = _swiglu_chunks(xn_ref, slot, w1_ref, w2_ref, between)
    out = x2_ref[slot] + 0.5 * _mod_slice(mod_ref, 8) * acc
    o_ref[...] = _rmsnorm(out, fg_ref[...])


def _ffn2_call(xr, y, mods3, g, w1, w2, final_g, *, rows, rows_per_mod):
    tm = FFN_ROW_TILE
    n_tiles = rows // tm
    assert n_tiles * tm == rows and rows_per_mod % tm == 0
    tpb = rows_per_mod // tm
    nxt = lambda i: jnp.minimum(i + 1, n_tiles - 1)
    resident = dict(pipeline_mode=pl.Buffered(1))
    cur_spec = pl.BlockSpec((tm, D_MODEL), lambda i: (i, 0))
    first_spec = pl.BlockSpec((tm, D_MODEL), lambda i: (0, 0), **resident)
    nxt_spec = pl.BlockSpec((tm, D_MODEL), lambda i: (nxt(i), 0))
    vec_spec = pl.BlockSpec((1, D_MODEL), lambda i: (0, 0))
    return pl.pallas_call(
        _ffn2_kernel,
        grid=(n_tiles,),
        in_specs=[
            first_spec, nxt_spec, first_spec, nxt_spec,
            pl.BlockSpec((1, 1, N_MOD * D_MODEL), lambda i: (i // tpb, 0, 0)),
            pl.BlockSpec((1, 1, N_MOD * D_MODEL), lambda i: (nxt(i) // tpb, 0, 0)),
            vec_spec,
            pl.BlockSpec(w1.shape, lambda i: (0, 0), **resident),
            pl.BlockSpec(w2.shape, lambda i: (0, 0), **resident),
            vec_spec,
        ],
        out_specs=cur_spec,
        out_shape=jax.ShapeDtypeStruct((rows, D_MODEL), F32),
        scratch_shapes=[pltpu.VMEM((2, tm, D_MODEL), BF16), pltpu.VMEM((2, tm, D_MODEL), F32)],
        compiler_params=pltpu.CompilerParams(
            dimension_semantics=("arbitrary",), vmem_limit_bytes=VMEM_LIMIT),
        name="ffn2",
    )(xr, xr, y, y, mods3, mods3, g, w1, w2, final_g)


_PROJ_COL = {"q": 0, "k": 1, "v": 2, "g": 3, "u": 4, "vg": 5}
_PROJ_ORDER = ("vg", "k", "q", "u", "g", "v")


def _proj_kernel(x_ref, mod_ref, g_ref, w_ref, *rest, outs, rope):
    if rope:
        cs_ref, sn_ref = rest[:2]
        rest = rest[2:]
    o_refs = dict(zip(outs, rest))
    y = _rmsnorm(x_ref[...], g_ref[...])
    xn = (y * (1 + _mod_slice(mod_ref, 4)) + _mod_slice(mod_ref, 3)).astype(BF16)
    k_scale = HEAD_DIM ** -0.5
    for name in sorted(outs, key=_PROJ_ORDER.index):
        col = _PROJ_COL[name] * RET_WIDTH
        p = _dot(xn, w_ref[:, col:col + RET_WIDTH])
        o_ref = o_refs[name]
        if name in ("q", "k", "v", "g"):
            for h in range(HEADS):
                t = p[:, h * HEAD_DIM:(h + 1) * HEAD_DIM]
                if name in ("q", "k") and rope:
                    t = t * cs_ref[...] + pltpu.roll(t, HEAD_DIM // 2, 1) * sn_ref[...]
                if name == "k":
                    t = t * k_scale
                if name == "g":
                    t = jax.nn.silu(t)
                if name == "k":
                    for cc in range(t.shape[0] // CHUNK):
                        o_ref[h, cc] = t[cc * CHUNK:(cc + 1) * CHUNK, :].T.astype(BF16)
                else:
                    o_ref[h] = t.astype(BF16)
        elif name == "u":
            o_ref[...] = jax.nn.gelu(p).astype(BF16)
        else:
            gv = jax.nn.gelu(p)
            for gi in range(GROUPS):
                t = gv[:, gi * GROUP_DIM:(gi + 1) * GROUP_DIM]
                mu = jnp.mean(t, axis=-1, keepdims=True)
                d = t - mu
                var = jnp.mean(d * d, axis=-1, keepdims=True)
                o_ref[:, gi * GROUP_DIM:(gi + 1) * GROUP_DIM] = (d * lax.rsqrt(var + EPS)).astype(BF16)


def _proj_call(xr, mods3, g, w_in, *, outs, row0, rows, tiles_per_batch, mod_row0, rope_tabs=None):
    tm = ROW_TILE
    assert row0 % tm == 0 and rows % tm == 0
    tile0 = row0 // tm
    mod_map = lambda i: (mod_row0 + i // tiles_per_batch, 0, 0)
    in_specs = [
        pl.BlockSpec((tm, D_MODEL), lambda i: (tile0 + i, 0)),
        pl.BlockSpec((1, 1, N_MOD * D_MODEL), mod_map),
        pl.BlockSpec((1, D_MODEL), lambda i: (0, 0)),
        pl.BlockSpec(w_in.shape, lambda i: (0, 0)),
    ]
    args = [xr, mods3, g, w_in]
    if rope_tabs is not None:
        tab_map = lambda i: (i % tiles_per_batch, 0)
        in_specs += [pl.BlockSpec((tm, HEAD_DIM), tab_map)] * 2
        args += list(rope_tabs)
    out_specs, out_shape = [], []
    for name in outs:
        if name == "k":
            out_specs.append(pl.BlockSpec((HEADS, tm // CHUNK, HEAD_DIM, CHUNK), lambda i: (0, i, 0, 0)))
            out_shape.append(jax.ShapeDtypeStruct((HEADS, rows // CHUNK, HEAD_DIM, CHUNK), BF16))
        elif name in ("q", "v", "g"):
            out_specs.append(pl.BlockSpec((HEADS, tm, HEAD_DIM), lambda i: (0, i, 0)))
            out_shape.append(jax.ShapeDtypeStruct((HEADS, rows, HEAD_DIM), BF16))
        else:
            out_specs.append(pl.BlockSpec((tm, GMLP_WIDTH), lambda i: (i, 0)))
            out_shape.append(jax.ShapeDtypeStruct((rows, GMLP_WIDTH), BF16))
    return pl.pallas_call(
        functools.partial(_proj_kernel, outs=outs, rope=rope_tabs is not None),
        grid=(rows // tm,),
        in_specs=in_specs,
        out_specs=out_specs,
        out_shape=out_shape,
        compiler_params=pltpu.CompilerParams(
            dimension_semantics=("parallel",), vmem_limit_bytes=VMEM_LIMIT),
        name="proj_x" if rope_tabs is not None else "proj_ctx",
    )(*args)


def _ret_kernel(rd_ref, ktc_ref, vc_ref, q_ref, kt_ref, v_ref, sg_ref, o_ref, u_ref, s_ref,
                *, n_chunks, n_ctx_chunks):
    L = CHUNK
    lg_f = -jnp.exp(rd_ref[0, 0, 0:1, :])
    lg_b = -jnp.exp(rd_ref[1, 0, 0:1, :])
    ii = lax.broadcasted_iota(jnp.int32, (L, L), 0).astype(F32)
    jj = lax.broadcasted_iota(jnp.int32, (L, L), 1).astype(F32)
    diff = ii - jj
    lower = diff >= 0
    upper = diff <= 0
    mask = (jnp.where(lower, jnp.exp(jnp.where(lower, diff, 0.0) * lg_f), 0.0)
            + jnp.where(upper, jnp.exp(jnp.where(upper, -diff, 0.0) * lg_b), 0.0))
    xi_f = jnp.exp((ii + 1.0) * lg_f)
    ze_f = jnp.exp((L - 1.0 - ii) * lg_f)
    xi_b = jnp.exp((L - ii) * lg_b)
    ze_b = jnp.exp(ii * lg_b)
    cd_f = jnp.exp(L * lg_f)
    cd_b = jnp.exp(L * lg_b)

    def kv_pair(kt, v):
        vf = v.astype(F32)
        vz = jnp.concatenate([(vf * ze_f).astype(BF16), (vf * ze_b).astype(BF16)], axis=1)
        return _dot(kt, vz)

    def chunk_rows(c):
        return pl.ds(pl.multiple_of(c * L, L), L)

    uc = [kv_pair(ktc_ref[0, c], vc_ref[0, c * L:(c + 1) * L, :]) for c in range(n_ctx_chunks)]
    s_f = jnp.zeros((HEAD_DIM, HEAD_DIM), F32)
    for c in range(n_ctx_chunks):
        s_f = cd_f * s_f + uc[c][:, :HEAD_DIM]
    s_b = jnp.zeros((HEAD_DIM, HEAD_DIM), F32)
    for c in reversed(range(n_ctx_chunks)):
        s_b = cd_b * s_b + uc[c][:, HEAD_DIM:]

    def u_body(c, carry):
        u_ref[c] = kv_pair(kt_ref[0, c], v_ref[0, chunk_rows(c), :])
        return carry

    lax.fori_loop(0, n_chunks, u_body, 0, unroll=RET_UNROLL)

    def f_body(c, s):
        s_ref[c, 0:HEAD_DIM, :] = s.astype(BF16)
        return cd_f * s + u_ref[c, :, 0:HEAD_DIM]

    lax.fori_loop(0, n_chunks, f_body, s_f, unroll=RET_UNROLL)

    def b_body(t, s):
        c = n_chunks - 1 - t
        s_ref[c, HEAD_DIM:2 * HEAD_DIM, :] = s.astype(BF16)
        return cd_b * s + u_ref[c, :, HEAD_DIM:2 * HEAD_DIM]

    lax.fori_loop(0, n_chunks, b_body, s_b, unroll=RET_UNROLL)

    def o_body(c, carry):
        rows = chunk_rows(c)
        q = q_ref[0, rows, :]
        qf = q.astype(F32)
        p = _dot(q, kt_ref[0, c])
        lhs = jnp.concatenate(
            [(p * mask).astype(BF16), (qf * xi_f).astype(BF16), (qf * xi_b).astype(BF16)], axis=1)
        rhs = jnp.concatenate([v_ref[0, rows, :], s_ref[c]], axis=0)
        o = _dot(lhs, rhs)
        y = o * lax.rsqrt(jnp.mean(o * o, axis=-1, keepdims=True) + EPS)
        o_ref[0, rows, :] = (y * sg_ref[0, rows, :].astype(F32)).astype(BF16)
        return carry

    lax.fori_loop(0, n_chunks, o_body, 0, unroll=RET_UNROLL)


def _ret_call(rd, ktc, vc, q, kt, v, sg, *, batch, seq, ctx_len):
    n_chunks = seq // CHUNK
    n_ctx_chunks = ctx_len // CHUNK
    x_spec = pl.BlockSpec((1, seq, HEAD_DIM), lambda b, h: (h, b, 0))
    kt_spec = pl.BlockSpec((1, n_chunks, HEAD_DIM, CHUNK), lambda b, h: (h, b, 0, 0))
    vc_spec = pl.BlockSpec((1, ctx_len, HEAD_DIM), lambda b, h: (h, b, 0))
    ktc_spec = pl.BlockSpec((1, n_ctx_chunks, HEAD_DIM, CHUNK), lambda b, h: (h, b, 0, 0))
    return pl.pallas_call(
        functools.partial(_ret_kernel, n_chunks=n_chunks, n_ctx_chunks=n_ctx_chunks),
        grid=(batch, HEADS),
        in_specs=[pl.BlockSpec((2, 1, 8, HEAD_DIM), lambda b, h: (0, h, 0, 0)),
                  ktc_spec, vc_spec, x_spec, kt_spec, x_spec, x_spec],
        out_specs=x_spec,
        out_shape=jax.ShapeDtypeStruct((HEADS, batch * seq, HEAD_DIM), BF16),
        scratch_shapes=[pltpu.VMEM((n_chunks, HEAD_DIM, 2 * HEAD_DIM), F32),
                        pltpu.VMEM((n_chunks, 2 * HEAD_DIM, HEAD_DIM), BF16)],
        compiler_params=pltpu.CompilerParams(
            dimension_semantics=("parallel", "parallel"), vmem_limit_bytes=VMEM_LIMIT),
        name="retention",
    )(rd, ktc, vc, q, kt, v, sg)


def _mix_kernel(ret_ref, gu_ref, vn_ref, ws_ref, bs_ref, wo_ref, o_ref, mix_ref):
    tm = o_ref.shape[0]
    for h in range(HEADS):
        mix_ref[:, h * HEAD_DIM:(h + 1) * HEAD_DIM] = ret_ref[h]
    for c in range(tm // CHUNK):
        rows = slice(c * CHUNK, (c + 1) * CHUNK)
        for gi in range(GROUPS):
            cols = slice(gi * GROUP_DIM, (gi + 1) * GROUP_DIM)
            mixed = _dot(ws_ref[gi].astype(BF16), vn_ref[rows, cols]) + bs_ref[gi]
            mix_ref[rows, RET_WIDTH + gi * GROUP_DIM:RET_WIDTH + (gi + 1) * GROUP_DIM] = (
                gu_ref[rows, cols].astype(F32) * mixed).astype(BF16)
    o_ref[...] = _dot(mix_ref[...], wo_ref[...]).astype(BF16)


def _mix_call(ret, gu, vn, ws, bs, w_out):
    rows = gu.shape[0]
    tm = ROW_TILE
    return pl.pallas_call(
        _mix_kernel,
        grid=(rows // tm,),
        in_specs=[
            pl.BlockSpec((HEADS, tm, HEAD_DIM), lambda i: (0, i, 0)),
            pl.BlockSpec((tm, GMLP_WIDTH), lambda i: (i, 0)),
            pl.BlockSpec((tm, GMLP_WIDTH), lambda i: (i, 0)),
            pl.BlockSpec(ws.shape, lambda i: (0, 0, 0)),
            pl.BlockSpec(bs.shape, lambda i: (0, 0, 0)),
            pl.BlockSpec(w_out.shape, lambda i: (0, 0)),
        ],
        out_specs=pl.BlockSpec((tm, D_MODEL), lambda i: (i, 0)),
        out_shape=jax.ShapeDtypeStruct((rows, D_MODEL), BF16),
        scratch_shapes=[pltpu.VMEM((tm, RET_WIDTH + GMLP_WIDTH), BF16)],
        compiler_params=pltpu.CompilerParams(
            dimension_semantics=("parallel",), vmem_limit_bytes=VMEM_LIMIT),
        name="mix",
    )(ret, gu, vn, ws, bs, w_out)


def _rope_tables(seq):
    rows = seq // GRID_W
    row = np.repeat(np.arange(rows, dtype=np.float64), GRID_W)
    col = np.tile(np.arange(GRID_W, dtype=np.float64), rows)
    n_freq = HEAD_DIM // 4
    freqs = ROPE_BASE ** (-np.arange(n_freq, dtype=np.float64) / n_freq)
    ang = np.concatenate([row[:, None] * freqs, col[:, None] * freqs], axis=-1)
    cos, sin = np.cos(ang), np.sin(ang)
    return (jnp.asarray(np.concatenate([cos, cos], axis=-1), F32),
            jnp.asarray(np.concatenate([-sin, sin], axis=-1), F32))


def kernel(x, c, ctx, c_ctx, w_ada, b_ada, norm_g, ffn1_w1, ffn1_w2, w_in, ret_decay,
           gmlp_ws, gmlp_bs, w_out, ffn2_w1, ffn2_w2, final_g):
    batch, seq, d = x.shape
    ctx_len = ctx.shape[1]
    assert w_ada.shape[0] == 1, "single trunk layer only"
    assert d == D_MODEL and seq % ROW_TILE == 0 and (batch * ctx_len) % ROW_TILE == 0
    assert batch + 1 <= MOD_ROWS
    tiles_per_batch = seq // ROW_TILE
    ctx_rows = batch * ctx_len

    xr = x.reshape(batch * seq, d)
    ctxr = ctx.reshape(ctx_rows, d)
    mods3 = _mods_call(c, c_ctx[None], w_ada[0], b_ada[0][None])

    g0, g1, g2 = norm_g[0, 0][None], norm_g[0, 1][None], norm_g[0, 2][None]
    x1, w_in_b, w_out_b, w1b, w2b = _ffn1_call(
        xr, ctxr, mods3, g0, ffn1_w1[0], ffn1_w2[0], rows_per_mod=seq, ctx_mod_row=batch,
        cast_weights=(w_in[0], w_out[0], ffn2_w1[0], ffn2_w2[0]))

    q, kt, v, sg, gu, vn = _proj_call(
        x1, mods3, g1, w_in_b, outs=("q", "k", "v", "g", "u", "vg"), row0=0, rows=batch * seq,
        tiles_per_batch=tiles_per_batch, mod_row0=0, rope_tabs=_rope_tables(seq))
    ktc, vc = _proj_call(x1, mods3, g1, w_in_b, outs=("k", "v"), row0=batch * seq, rows=ctx_rows,
                        tiles_per_batch=ctx_rows // ROW_TILE, mod_row0=batch)

    rd = jnp.broadcast_to(ret_decay[0].astype(F32)[:, :, None, None], (2, HEADS, 8, HEAD_DIM))
    ret = _ret_call(rd, ktc, vc, q, kt, v, sg, batch=batch, seq=seq, ctx_len=ctx_len)

    bs = gmlp_bs[0][:, :, None]
    y = _mix_call(ret, gu, vn, gmlp_ws[0], bs, w_out_b)
    out = _ffn2_call(x1, y, mods3, g2, w1b, w2b, final_g[None], rows=batch * seq, rows_per_mod=seq)
    return out.reshape(batch, seq, d)
```

```python
import functools

import jax
import jax.numpy as jnp
import numpy as np
from jax import lax
from jax.experimental import pallas as pl
from jax.experimental.pallas import tpu as pltpu

F32 = jnp.float32
BF16 = jnp.bfloat16

D_MODEL = 1024
CHUNK = 128
HEADS = 4
HEAD_DIM = 128
GROUPS = 4
GROUP_DIM = 128
RET_WIDTH = HEADS * HEAD_DIM
GMLP_WIDTH = GROUPS * GROUP_DIM
D_FF = 2816
N_MOD = 9
GRID_W = 64
ROPE_BASE = 10000.0
EPS = 1e-6

ROW_TILE = 1024
FF_TILE = 256
BF16_ROWS = 16
MOD_ROWS = 8
MODS_K_TILE = 128
RET_UNROLL = 32
VMEM_LIMIT = 56 * 1024 * 1024


def _rmsnorm(x, g):
    return x * lax.rsqrt(jnp.mean(x * x, axis=-1, keepdims=True) + EPS) * g


def _mod_slice(mod_ref, k):
    return mod_ref[0, :, k * D_MODEL:(k + 1) * D_MODEL]


def _dot(a, b):
    return jnp.dot(a, b, preferred_element_type=F32)


def _mods_kernel(c_ref, cctx_ref, w_ref, b_ref, o_ref, s_ref):
    batch = c_ref.shape[0]
    k = pl.program_id(0)
    tk = w_ref.shape[0]

    @pl.when(k == 0)
    def _():
        s_ref[...] = jnp.zeros_like(s_ref)
        for kk in range(s_ref.shape[0]):
            cols = slice(kk * tk, (kk + 1) * tk)
            s_ref[kk, 0:batch, :] = jax.nn.silu(c_ref[:, cols])
            s_ref[kk, batch:batch + 1, :] = jax.nn.silu(cctx_ref[:, cols])
        o_ref[:, 0, :] = jnp.broadcast_to(b_ref[...], (MOD_ROWS, b_ref.shape[1]))

    o_ref[:, 0, :] += _dot(s_ref[k].astype(BF16), w_ref[...].astype(BF16))


def _mods_call(c, c_ctx, w_ada, b_ada):
    d, n_out = w_ada.shape
    tk = MODS_K_TILE
    return pl.pallas_call(
        _mods_kernel,
        grid=(d // tk,),
        in_specs=[
            pl.BlockSpec(c.shape, lambda k: (0, 0)),
            pl.BlockSpec(c_ctx.shape, lambda k: (0, 0)),
            pl.BlockSpec((tk, n_out), lambda k: (k, 0)),
            pl.BlockSpec((1, n_out), lambda k: (0, 0)),
        ],
        out_specs=pl.BlockSpec((MOD_ROWS, 1, n_out), lambda k: (0, 0, 0)),
        out_shape=jax.ShapeDtypeStruct((MOD_ROWS, 1, n_out), F32),
        scratch_shapes=[pltpu.VMEM((d // tk, MOD_ROWS, tk), F32)],
        compiler_params=pltpu.CompilerParams(
            dimension_semantics=("arbitrary",), vmem_limit_bytes=VMEM_LIMIT),
        name="mods",
    )(c, c_ctx, w_ada, b_ada)


N_FF_CHUNKS = D_FF // FF_TILE
FFN_ROW_TILE = 512
W_LOAD_CHUNKS = 16
W_LOAD_SLOTS = 4


def _swiglu_chunks(xn_ref, slot, w1_ref, w2_ref, between):
    acc = None
    for s in range(N_FF_CHUNKS):
        cols = slice(s * FF_TILE, (s + 1) * FF_TILE)
        gate = _dot(xn_ref[slot], w1_ref[:, cols])
        up = _dot(xn_ref[slot], w1_ref[:, D_FF + s * FF_TILE:D_FF + (s + 1) * FF_TILE])
        h = (jax.nn.silu(gate) * up).astype(BF16)
        d = _dot(h, w2_ref[cols, :])
        acc = d if acc is None else acc + d
        between(s)
    return acc


def _piece(tm, s):
    n = -(-tm // (N_FF_CHUNKS * BF16_ROWS)) * BF16_ROWS
    return min(s * n, tm - n), n


def _ffn1_kernel(x_ref, xnext_ref, c_ref, mod_ref, modn_ref, g_ref, w1_hbm, w2_hbm, *rest,
                 n_cast, n_x_tiles, n_c_tiles):
    rest = list(rest)
    cast_in = [rest.pop(0) for _ in range(n_cast)]
    o_ref = rest.pop(0)
    cast_out = [rest.pop(0) for _ in range(n_cast)]
    xn_ref, w1_ref, w2_ref, st1_ref, st2_ref, sem1, sem2 = rest
    i = pl.program_id(0)
    slot = i % 2
    tm = o_ref.shape[0]

    def tile_rows(j, lat_ref, r0, n):
        c0 = jnp.clip(j - n_x_tiles, 0, n_c_tiles - 1) * tm + r0
        ctx_rows = c_ref[pl.ds(pl.multiple_of(c0, BF16_ROWS), n), :]
        return jnp.where(j < n_x_tiles, lat_ref[r0:r0 + n, :], ctx_rows)

    def normalized(x, m_ref):
        y = _rmsnorm(x, g_ref[...])
        return (y * (1 + _mod_slice(m_ref, 1)) + _mod_slice(m_ref, 0)).astype(BF16)

    def load_weight(hbm, dst, stage, sem):
        rows = dst.shape[0] // W_LOAD_CHUNKS

        def copy(c):
            s = c % W_LOAD_SLOTS
            return pltpu.make_async_copy(hbm.at[pl.ds(c * rows, rows), :], stage.at[s], sem.at[s])

        for c in range(W_LOAD_SLOTS - 1):
            copy(c).start()
        for c in range(W_LOAD_CHUNKS):
            if c + W_LOAD_SLOTS - 1 < W_LOAD_CHUNKS:
                copy(c + W_LOAD_SLOTS - 1).start()
            copy(c).wait()
            dst[c * rows:(c + 1) * rows, :] = stage[c % W_LOAD_SLOTS].astype(BF16)

    @pl.when(i == 0)
    def _():
        load_weight(w1_hbm, w1_ref, st1_ref, sem1)
        load_weight(w2_hbm, w2_ref, st2_ref, sem2)
        xn_ref[0] = normalized(x_ref[...], mod_ref)

    def between(s):
        r0, n = _piece(tm, s)
        xn_ref[1 - slot, r0:r0 + n, :] = normalized(tile_rows(i + 1, xnext_ref, r0, n), modn_ref)
        if s < n_cast:
            cast_out[s][...] = cast_in[s][...].astype(BF16)

    acc = _swiglu_chunks(xn_ref, slot, w1_ref, w2_ref, between)
    o_ref[...] = tile_rows(i, x_ref, 0, tm) + 0.5 * _mod_slice(mod_ref, 2) * acc


def _ffn1_call(xr, ctxr, mods3, g, w1, w2, *, rows_per_mod, ctx_mod_row, cast_weights):
    tm = FFN_ROW_TILE
    n_x, n_c = xr.shape[0] // tm, ctxr.shape[0] // tm
    assert n_x * tm == xr.shape[0] and n_c * tm == ctxr.shape[0] and rows_per_mod % tm == 0
    n_tiles = n_x + n_c
    tpb = rows_per_mod // tm
    lat = lambda j: jnp.minimum(j, n_x - 1)
    nxt = lambda i: jnp.minimum(i + 1, n_tiles - 1)
    mod_row = lambda j: jnp.where(j < n_x, j // tpb, ctx_mod_row)
    cast_specs, cast_shapes = [], []
    for w in cast_weights:
        per = 1 if (w.shape[0] // n_x) % BF16_ROWS == 0 else 2
        slab = w.shape[0] * per // n_x
        assert slab * n_x == w.shape[0] * per and slab % BF16_ROWS == 0, (w.shape, n_x)
        cast_specs.append(pl.BlockSpec((slab, w.shape[1]), lambda i, per=per: (lat(i) // per, 0)))
        cast_shapes.append(jax.ShapeDtypeStruct(w.shape, BF16))
    st1_rows, st2_rows = w1.shape[0] // W_LOAD_CHUNKS, w2.shape[0] // W_LOAD_CHUNKS
    assert st1_rows % BF16_ROWS == 0 and st2_rows % BF16_ROWS == 0
    in_specs = [
        pl.BlockSpec((tm, D_MODEL), lambda i: (lat(i), 0)),
        pl.BlockSpec((tm, D_MODEL), lambda i: (lat(i + 1), 0)),
        pl.BlockSpec(ctxr.shape, lambda i: (0, 0), pipeline_mode=pl.Buffered(1)),
        pl.BlockSpec((1, 1, N_MOD * D_MODEL), lambda i: (mod_row(i), 0, 0)),
        pl.BlockSpec((1, 1, N_MOD * D_MODEL), lambda i: (mod_row(nxt(i)), 0, 0)),
        pl.BlockSpec((1, D_MODEL), lambda i: (0, 0)),
        pl.BlockSpec(memory_space=pl.ANY),
        pl.BlockSpec(memory_space=pl.ANY),
    ]
    return pl.pallas_call(
        functools.partial(_ffn1_kernel, n_cast=len(cast_weights), n_x_tiles=n_x, n_c_tiles=n_c),
        grid=(n_tiles,),
        in_specs=in_specs + cast_specs,
        out_specs=[pl.BlockSpec((tm, D_MODEL), lambda i: (i, 0))] + cast_specs,
        out_shape=[jax.ShapeDtypeStruct((n_tiles * tm, D_MODEL), F32)] + cast_shapes,
        scratch_shapes=[pltpu.VMEM((2, tm, D_MODEL), BF16),
                        pltpu.VMEM(w1.shape, BF16), pltpu.VMEM(w2.shape, BF16),
                        pltpu.VMEM((W_LOAD_SLOTS, st1_rows, w1.shape[1]), F32),
                        pltpu.VMEM((W_LOAD_SLOTS, st2_rows, w2.shape[1]), F32),
                        pltpu.SemaphoreType.DMA((W_LOAD_SLOTS,)),
                        pltpu.SemaphoreType.DMA((W_LOAD_SLOTS,))],
        compiler_params=pltpu.CompilerParams(
            dimension_semantics=("arbitrary",), vmem_limit_bytes=VMEM_LIMIT),
        name="ffn1",
    )(xr, xr, ctxr, mods3, mods3, g, w1, w2, *cast_weights)


def _ffn2_kernel(x_ref, xnext_ref, y_ref, ynext_ref, mod_ref, modn_ref, g_ref, w1_ref, w2_ref, fg_ref,
                 o_ref, xn_ref, x2_ref):
    i = pl.program_id(0)
    slot = i % 2
    tm = o_ref.shape[0]

    def x2(xr, yr, m_ref, rows=slice(None)):
        return xr[rows, :] + _mod_slice(m_ref, 5) * yr[rows, :].astype(F32)

    def normalized(x, m_ref):
        y = _rmsnorm(x, g_ref[...])
        return (y * (1 + _mod_slice(m_ref, 7)) + _mod_slice(m_ref, 6)).astype(BF16)

    @pl.when(i == 0)
    def _():
        t = x2(x_ref, y_ref, mod_ref)
        x2_ref[0] = t
        xn_ref[0] = normalized(t, mod_ref)

    def between(s):
        r0, n = _piece(tm, s)
        t = x2(xnext_ref, ynext_ref, modn_ref, slice(r0, r0 + n))
        x2_ref[1 - slot, r0:r0 + n, :] = t
        xn_ref[1 - slot, r0:r0 + n, :] = normalized(t, modn_ref)

    acc = _swiglu_chunks(xn_ref, slot, w1_ref, w2_ref, between)
    out = x2_ref[slot] + 0.5 * _mod_slice(mod_ref, 8) * acc
    o_ref[...] = _rmsnorm(out, fg_ref[...])


def _ffn2_call(xr, y, mods3, g, w1, w2, final_g, *, rows, rows_per_mod):
    tm = FFN_ROW_TILE
    n_tiles = rows // tm
    assert n_tiles * tm == rows and rows_per_mod % tm == 0
    tpb = rows_per_mod // tm
    nxt = lambda i: jnp.minimum(i + 1, n_tiles - 1)
    resident = dict(pipeline_mode=pl.Buffered(1))
    cur_spec = pl.BlockSpec((tm, D_MODEL), lambda i: (i, 0))
    first_spec = pl.BlockSpec((tm, D_MODEL), lambda i: (0, 0), **resident)
    nxt_spec = pl.BlockSpec((tm, D_MODEL), lambda i: (nxt(i), 0))
    vec_spec = pl.BlockSpec((1, D_MODEL), lambda i: (0, 0))
    return pl.pallas_call(
        _ffn2_kernel,
        grid=(n_tiles,),
        in_specs=[
            first_spec, nxt_spec, first_spec, nxt_spec,
            pl.BlockSpec((1, 1, N_MOD * D_MODEL), lambda i: (i // tpb, 0, 0)),
            pl.BlockSpec((1, 1, N_MOD * D_MODEL), lambda i: (nxt(i) // tpb, 0, 0)),
            vec_spec,
            pl.BlockSpec(w1.shape, lambda i: (0, 0), **resident),
            pl.BlockSpec(w2.shape, lambda i: (0, 0), **resident),
            vec_spec,
        ],
        out_specs=cur_spec,
        out_shape=jax.ShapeDtypeStruct((rows, D_MODEL), F32),
        scratch_shapes=[pltpu.VMEM((2, tm, D_MODEL), BF16), pltpu.VMEM((2, tm, D_MODEL), F32)],
        compiler_params=pltpu.CompilerParams(
            dimension_semantics=("arbitrary",), vmem_limit_bytes=VMEM_LIMIT),
        name="ffn2",
    )(xr, xr, y, y, mods3, mods3, g, w1, w2, final_g)


_PROJ_COL = {"q": 0, "k": 1, "v": 2, "g": 3, "u": 4, "vg": 5}
_PROJ_ORDER = ("vg", "k", "q", "u", "g", "v")


def _proj_kernel(x_ref, mod_ref, g_ref, w_ref, *rest, outs, rope):
    if rope:
        cs_ref, sn_ref = rest[:2]
        rest = rest[2:]
    o_refs = dict(zip(outs, rest))
    y = _rmsnorm(x_ref[...], g_ref[...])
    xn = (y * (1 + _mod_slice(mod_ref, 4)) + _mod_slice(mod_ref, 3)).astype(BF16)
    k_scale = HEAD_DIM ** -0.5
    for name in sorted(outs, key=_PROJ_ORDER.index):
        col = _PROJ_COL[name] * RET_WIDTH
        p = _dot(xn, w_ref[:, col:col + RET_WIDTH])
        o_ref = o_refs[name]
        if name in ("q", "k", "v", "g"):
            for h in range(HEADS):
                t = p[:, h * HEAD_DIM:(h + 1) * HEAD_DIM]
                if name in ("q", "k") and rope:
                    t = t * cs_ref[...] + pltpu.roll(t, HEAD_DIM // 2, 1) * sn_ref[...]
                if name == "k":
                    t = t * k_scale
                if name == "g":
                    t = jax.nn.silu(t)
                if name == "k":
                    for cc in range(t.shape[0] // CHUNK):
                        o_ref[h, cc] = t[cc * CHUNK:(cc + 1) * CHUNK, :].T.astype(BF16)
                else:
                    o_ref[h] = t.astype(BF16)
        elif name == "u":
            o_ref[...] = jax.nn.gelu(p).astype(BF16)
        else:
            gv = jax.nn.gelu(p)
            for gi in range(GROUPS):
                t = gv[:, gi * GROUP_DIM:(gi + 1) * GROUP_DIM]
                mu = jnp.mean(t, axis=-1, keepdims=True)
                d = t - mu
                var = jnp.mean(d * d, axis=-1, keepdims=True)
                o_ref[:, gi * GROUP_DIM:(gi + 1) * GROUP_DIM] = (d * lax.rsqrt(var + EPS)).astype(BF16)


def _proj_call(xr, mods3, g, w_in, *, outs, row0, rows, tiles_per_batch, mod_row0, rope_tabs=None):
    tm = ROW_TILE
    assert row0 % tm == 0 and rows % tm == 0
    tile0 = row0 // tm
    mod_map = lambda i: (mod_row0 + i // tiles_per_batch, 0, 0)
    in_specs = [
        pl.BlockSpec((tm, D_MODEL), lambda i: (tile0 + i, 0)),
        pl.BlockSpec((1, 1, N_MOD * D_MODEL), mod_map),
        pl.BlockSpec((1, D_MODEL), lambda i: (0, 0)),
        pl.BlockSpec(w_in.shape, lambda i: (0, 0)),
    ]
    args = [xr, mods3, g, w_in]
    if rope_tabs is not None:
        tab_map = lambda i: (i % tiles_per_batch, 0)
        in_specs += [pl.BlockSpec((tm, HEAD_DIM), tab_map)] * 2
        args += list(rope_tabs)
    out_specs, out_shape = [], []
    for name in outs:
        if name == "k":
            out_specs.append(pl.BlockSpec((HEADS, tm // CHUNK, HEAD_DIM, CHUNK), lambda i: (0, i, 0, 0)))
            out_shape.append(jax.ShapeDtypeStruct((HEADS, rows // CHUNK, HEAD_DIM, CHUNK), BF16))
        elif name in ("q", "v", "g"):
            out_specs.append(pl.BlockSpec((HEADS, tm, HEAD_DIM), lambda i: (0, i, 0)))
            out_shape.append(jax.ShapeDtypeStruct((HEADS, rows, HEAD_DIM), BF16))
        else:
            out_specs.append(pl.BlockSpec((tm, GMLP_WIDTH), lambda i: (i, 0)))
            out_shape.append(jax.ShapeDtypeStruct((rows, GMLP_WIDTH), BF16))
    return pl.pallas_call(
        functools.partial(_proj_kernel, outs=outs, rope=rope_tabs is not None),
        grid=(rows // tm,),
        in_specs=in_specs,
        out_specs=out_specs,
        out_shape=out_shape,
        compiler_params=pltpu.CompilerParams(
            dimension_semantics=("parallel",), vmem_limit_bytes=VMEM_LIMIT),
        name="proj_x" if rope_tabs is not None else "proj_ctx",
    )(*args)


def _ret_kernel(rd_ref, ktc_ref, vc_ref, q_ref, kt_ref, v_ref, sg_ref, o_ref, u_ref, s_ref,
                *, n_chunks, n_ctx_chunks):
    L = CHUNK
    lg_f = -jnp.exp(rd_ref[0, 0, 0:1, :])
    lg_b = -jnp.exp(rd_ref[1, 0, 0:1, :])
    ii = lax.broadcasted_iota(jnp.int32, (L, L), 0).astype(F32)
    jj = lax.broadcasted_iota(jnp.int32, (L, L), 1).astype(F32)
    diff = ii - jj
    lower = diff >= 0
    upper = diff <= 0
    mask = (jnp.where(lower, jnp.exp(jnp.where(lower, diff, 0.0) * lg_f), 0.0)
            + jnp.where(upper, jnp.exp(jnp.where(upper, -diff, 0.0) * lg_b), 0.0))
    xi_f = jnp.exp((ii + 1.0) * lg_f)
    ze_f = jnp.exp((L - 1.0 - ii) * lg_f)
    xi_b = jnp.exp((L - ii) * lg_b)
    ze_b = jnp.exp(ii * lg_b)
    cd_f = jnp.exp(L * lg_f)
    cd_b = jnp.exp(L * lg_b)

    def kv_pair(kt, v):
        vf = v.astype(F32)
        vz = jnp.concatenate([(vf * ze_f).astype(BF16), (vf * ze_b).astype(BF16)], axis=1)
        return _dot(kt, vz)

    def chunk_rows(c):
        return pl.ds(pl.multiple_of(c * L, L), L)

    uc = [kv_pair(ktc_ref[0, c], vc_ref[0, c * L:(c + 1) * L, :]) for c in range(n_ctx_chunks)]
    s_f = jnp.zeros((HEAD_DIM, HEAD_DIM), F32)
    for c in range(n_ctx_chunks):
        s_f = cd_f * s_f + uc[c][:, :HEAD_DIM]
    s_b = jnp.zeros((HEAD_DIM, HEAD_DIM), F32)
    for c in reversed(range(n_ctx_chunks)):
        s_b = cd_b * s_b + uc[c][:, HEAD_DIM:]

    def u_body(c, carry):
        u_ref[c] = kv_pair(kt_ref[0, c], v_ref[0, chunk_rows(c), :])
        return carry

    lax.fori_loop(0, n_chunks, u_body, 0, unroll=RET_UNROLL)

    def f_body(c, s):
        s_ref[c, 0:HEAD_DIM, :] = s.astype(BF16)
        return cd_f * s + u_ref[c, :, 0:HEAD_DIM]

    lax.fori_loop(0, n_chunks, f_body, s_f, unroll=RET_UNROLL)

    def b_body(t, s):
        c = n_chunks - 1 - t
        s_ref[c, HEAD_DIM:2 * HEAD_DIM, :] = s.astype(BF16)
        return cd_b * s + u_ref[c, :, HEAD_DIM:2 * HEAD_DIM]

    lax.fori_loop(0, n_chunks, b_body, s_b, unroll=RET_UNROLL)

    def o_body(c, carry):
        rows = chunk_rows(c)
        q = q_ref[0, rows, :]
        qf = q.astype(F32)
        p = _dot(q, kt_ref[0, c])
        lhs = jnp.concatenate(
            [(p * mask).astype(BF16), (qf * xi_f).astype(BF16), (qf * xi_b).astype(BF16)], axis=1)
        rhs = jnp.concatenate([v_ref[0, rows, :], s_ref[c]], axis=0)
        o = _dot(lhs, rhs)
        y = o * lax.rsqrt(jnp.mean(o * o, axis=-1, keepdims=True) + EPS)
        o_ref[0, rows, :] = (y * sg_ref[0, rows, :].astype(F32)).astype(BF16)
        return carry

    lax.fori_loop(0, n_chunks, o_body, 0, unroll=RET_UNROLL)


def _ret_call(rd, ktc, vc, q, kt, v, sg, *, batch, seq, ctx_len):
    n_chunks = seq // CHUNK
    n_ctx_chunks = ctx_len // CHUNK
    x_spec = pl.BlockSpec((1, seq, HEAD_DIM), lambda b, h: (h, b, 0))
    kt_spec = pl.BlockSpec((1, n_chunks, HEAD_DIM, CHUNK), lambda b, h: (h, b, 0, 0))
    vc_spec = pl.BlockSpec((1, ctx_len, HEAD_DIM), lambda b, h: (h, b, 0))
    ktc_spec = pl.BlockSpec((1, n_ctx_chunks, HEAD_DIM, CHUNK), lambda b, h: (h, b, 0, 0))
    return pl.pallas_call(
        functools.partial(_ret_kernel, n_chunks=n_chunks, n_ctx_chunks=n_ctx_chunks),
        grid=(batch, HEADS),
        in_specs=[pl.BlockSpec((2, 1, 8, HEAD_DIM), lambda b, h: (0, h, 0, 0)),
                  ktc_spec, vc_spec, x_spec, kt_spec, x_spec, x_spec],
        out_specs=x_spec,
        out_shape=jax.ShapeDtypeStruct((HEADS, batch * seq, HEAD_DIM), BF16),
        scratch_shapes=[pltpu.VMEM((n_chunks, HEAD_DIM, 2 * HEAD_DIM), F32),
                        pltpu.VMEM((n_chunks, 2 * HEAD_DIM, HEAD_DIM), BF16)],
        compiler_params=pltpu.CompilerParams(
            dimension_semantics=("parallel", "parallel"), vmem_limit_bytes=VMEM_LIMIT),
        name="retention",
    )(rd, ktc, vc, q, kt, v, sg)


def _mix_kernel(ret_ref, gu_ref, vn_ref, ws_ref, bs_ref, wo_ref, o_ref, mix_ref):
    tm = o_ref.shape[0]
    for h in range(HEADS):
        mix_ref[:, h * HEAD_DIM:(h + 1) * HEAD_DIM] = ret_ref[h]
    for c in range(tm // CHUNK):
        rows = slice(c * CHUNK, (c + 1) * CHUNK)
        for gi in range(GROUPS):
            cols = slice(gi * GROUP_DIM, (gi + 1) * GROUP_DIM)
            mixed = _dot(ws_ref[gi].astype(BF16), vn_ref[rows, cols]) + bs_ref[gi]
            mix_ref[rows, RET_WIDTH + gi * GROUP_DIM:RET_WIDTH + (gi + 1) * GROUP_DIM] = (
                gu_ref[rows, cols].astype(F32) * mixed).astype(BF16)
    o_ref[...] = _dot(mix_ref[...], wo_ref[...]).astype(BF16)


def _mix_call(ret, gu, vn, ws, bs, w_out):
    rows = gu.shape[0]
    tm = ROW_TILE
    return pl.pallas_call(
        _mix_kernel,
        grid=(rows // tm,),
        in_specs=[
            pl.BlockSpec((HEADS, tm, HEAD_DIM), lambda i: (0, i, 0)),
            pl.BlockSpec((tm, GMLP_WIDTH), lambda i: (i, 0)),
            pl.BlockSpec((tm, GMLP_WIDTH), lambda i: (i, 0)),
            pl.BlockSpec(ws.shape, lambda i: (0, 0, 0)),
            pl.BlockSpec(bs.shape, lambda i: (0, 0, 0)),
            pl.BlockSpec(w_out.shape, lambda i: (0, 0)),
        ],
        out_specs=pl.BlockSpec((tm, D_MODEL), lambda i: (i, 0)),
        out_shape=jax.ShapeDtypeStruct((rows, D_MODEL), BF16),
        scratch_shapes=[pltpu.VMEM((tm, RET_WIDTH + GMLP_WIDTH), BF16)],
        compiler_params=pltpu.CompilerParams(
            dimension_semantics=("parallel",), vmem_limit_bytes=VMEM_LIMIT),
        name="mix",
    )(ret, gu, vn, ws, bs, w_out)


def _rope_tables(seq):
    rows = seq // GRID_W
    row = np.repeat(np.arange(rows, dtype=np.float64), GRID_W)
    col = np.tile(np.arange(GRID_W, dtype=np.float64), rows)
    n_freq = HEAD_DIM // 4
    freqs = ROPE_BASE ** (-np.arange(n_freq, dtype=np.float64) / n_freq)
    ang = np.concatenate([row[:, None] * freqs, col[:, None] * freqs], axis=-1)
    cos, sin = np.cos(ang), np.sin(ang)
    return (jnp.asarray(np.concatenate([cos, cos], axis=-1), F32),
            jnp.asarray(np.concatenate([-sin, sin], axis=-1), F32))


def kernel(x, c, ctx, c_ctx, w_ada, b_ada, norm_g, ffn1_w1, ffn1_w2, w_in, ret_decay,
           gmlp_ws, gmlp_bs, w_out, ffn2_w1, ffn2_w2, final_g):
    batch, seq, d = x.shape
    ctx_len = ctx.shape[1]
    assert w_ada.shape[0] == 1, "single trunk layer only"
    assert d == D_MODEL and seq % ROW_TILE == 0 and (batch * ctx_len) % ROW_TILE == 0
    assert batch + 1 <= MOD_ROWS
    tiles_per_batch = seq // ROW_TILE
    ctx_rows = batch * ctx_len

    xr = x.reshape(batch * seq, d)
    ctxr = ctx.reshape(ctx_rows, d)
    mods3 = _mods_call(c, c_ctx[None], w_ada[0], b_ada[0][None])

    g0, g1, g2 = norm_g[0, 0][None], norm_g[0, 1][None], norm_g[0, 2][None]
    x1, w_in_b, w_out_b, w1b, w2b = _ffn1_call(
        xr, ctxr, mods3, g0, ffn1_w1[0], ffn1_w2[0], rows_per_mod=seq, ctx_mod_row=batch,
        cast_weights=(w_in[0], w_out[0], ffn2_w1[0], ffn2_w2[0]))

    q, kt, v, sg, gu, vn = _proj_call(
        x1, mods3, g1, w_in_b, outs=("q", "k", "v", "g", "u", "vg"), row0=0, rows=batch * seq,
        tiles_per_batch=tiles_per_batch, mod_row0=0, rope_tabs=_rope_tables(seq))
    ktc, vc = _proj_call(x1, mods3, g1, w_in_b, outs=("k", "v"), row0=batch * seq, rows=ctx_rows,
                        tiles_per_batch=ctx_rows // ROW_TILE, mod_row0=batch)

    rd = jnp.broadcast_to(ret_decay[0].astype(F32)[:, :, None, None], (2, HEADS, 8, HEAD_DIM))
    ret = _ret_call(rd, ktc, vc, q, kt, v, sg, batch=batch, seq=seq, ctx_len=ctx_len)

    bs = gmlp_bs[0][:, :, None]
    y = _mix_call(ret, gu, vn, gmlp_ws[0], bs, w_out_b)
    out = _ffn2_call(x1, y, mods3, g2, w1b, w2b, final_g[None], rows=batch * seq, rows_per_mod=seq)
    return out.reshape(batch, seq, d)
```

```python
import functools

import jax
import jax.numpy as jnp
import numpy as np
from jax import lax
from jax.experimental import pallas as pl
from jax.experimental.pallas import tpu as pltpu

F32 = jnp.float32
BF16 = jnp.bfloat16

D_MODEL = 1024
CHUNK = 128
HEADS = 4
HEAD_DIM = 128
GROUPS = 4
GROUP_DIM = 128
RET_WIDTH = HEADS * HEAD_DIM
GMLP_WIDTH = GROUPS * GROUP_DIM
D_FF = 2816
N_MOD = 9
GRID_W = 64
ROPE_BASE = 10000.0
EPS = 1e-6

ROW_TILE = 1024
MIX_ROW_TILE = 2048
FF_TILE = 256
BF16_ROWS = 16
MOD_ROWS = 8
MODS_K_TILE = 128
RET_UNROLL = 32
VMEM_LIMIT = 56 * 1024 * 1024


def _rmsnorm(x, g):
    return x * lax.rsqrt(jnp.mean(x * x, axis=-1, keepdims=True) + EPS) * g


def _mod_slice(mod_ref, k):
    return mod_ref[0, :, k * D_MODEL:(k + 1) * D_MODEL]


def _dot(a, b):
    return jnp.dot(a, b, preferred_element_type=F32)


def _mods_kernel(c_ref, cctx_ref, w_ref, b_ref, o_ref, s_ref):
    batch = c_ref.shape[0]
    k = pl.program_id(0)
    tk = w_ref.shape[0]

    @pl.when(k == 0)
    def _():
        s_ref[...] = jnp.zeros_like(s_ref)
        for kk in range(s_ref.shape[0]):
            cols = slice(kk * tk, (kk + 1) * tk)
            s_ref[kk, 0:batch, :] = jax.nn.silu(c_ref[:, cols])
            s_ref[kk, batch:batch + 1, :] = jax.nn.silu(cctx_ref[:, cols])
        o_ref[:, 0, :] = jnp.broadcast_to(b_ref[...], (MOD_ROWS, b_ref.shape[1]))

    o_ref[:, 0, :] += _dot(s_ref[k].astype(BF16), w_ref[...].astype(BF16))


def _mods_call(c, c_ctx, w_ada, b_ada):
    d, n_out = w_ada.shape
    tk = MODS_K_TILE
    return pl.pallas_call(
        _mods_kernel,
        grid=(d // tk,),
        in_specs=[
            pl.BlockSpec(c.shape, lambda k: (0, 0)),
            pl.BlockSpec(c_ctx.shape, lambda k: (0, 0)),
            pl.BlockSpec((tk, n_out), lambda k: (k, 0)),
            pl.BlockSpec((1, n_out), lambda k: (0, 0)),
        ],
        out_specs=pl.BlockSpec((MOD_ROWS, 1, n_out), lambda k: (0, 0, 0)),
        out_shape=jax.ShapeDtypeStruct((MOD_ROWS, 1, n_out), F32),
        scratch_shapes=[pltpu.VMEM((d // tk, MOD_ROWS, tk), F32)],
        compiler_params=pltpu.CompilerParams(
            dimension_semantics=("arbitrary",), vmem_limit_bytes=VMEM_LIMIT),
        name="mods",
    )(c, c_ctx, w_ada, b_ada)


N_FF_CHUNKS = D_FF // FF_TILE
FFN_ROW_TILE = 512
W_LOAD_CHUNKS = 16
W_LOAD_SLOTS = 4


def _swiglu_chunks(xn_ref, slot, w1_ref, w2_ref, between):
    acc = None
    for s in range(N_FF_CHUNKS):
        cols = slice(s * FF_TILE, (s + 1) * FF_TILE)
        gate = _dot(xn_ref[slot], w1_ref[:, cols])
        up = _dot(xn_ref[slot], w1_ref[:, D_FF + s * FF_TILE:D_FF + (s + 1) * FF_TILE])
        h = (jax.nn.silu(gate) * up).astype(BF16)
        d = _dot(h, w2_ref[cols, :])
        acc = d if acc is None else acc + d
        between(s)
    return acc


def _piece(tm, s):
    n = -(-tm // (N_FF_CHUNKS * BF16_ROWS)) * BF16_ROWS
    return min(s * n, tm - n), n


def _ffn1_kernel(x_ref, xnext_ref, c_ref, mod_ref, modn_ref, g_ref, w1_hbm, w2_hbm, *rest,
                 n_cast, n_x_tiles, n_c_tiles):
    rest = list(rest)
    cast_in = [rest.pop(0) for _ in range(n_cast)]
    o_ref = rest.pop(0)
    cast_out = [rest.pop(0) for _ in range(n_cast)]
    xn_ref, w1_ref, w2_ref, st1_ref, st2_ref, sem1, sem2 = rest
    i = pl.program_id(0)
    slot = i % 2
    tm = o_ref.shape[0]

    def tile_rows(j, lat_ref, r0, n):
        c0 = jnp.clip(j - n_x_tiles, 0, n_c_tiles - 1) * tm + r0
        ctx_rows = c_ref[pl.ds(pl.multiple_of(c0, BF16_ROWS), n), :]
        return jnp.where(j < n_x_tiles, lat_ref[r0:r0 + n, :], ctx_rows)

    def normalized(x, m_ref):
        y = _rmsnorm(x, g_ref[...])
        return (y * (1 + _mod_slice(m_ref, 1)) + _mod_slice(m_ref, 0)).astype(BF16)

    def load_weight(hbm, dst, stage, sem):
        rows = dst.shape[0] // W_LOAD_CHUNKS

        def copy(c):
            s = c % W_LOAD_SLOTS
            return pltpu.make_async_copy(hbm.at[pl.ds(c * rows, rows), :], stage.at[s], sem.at[s])

        for c in range(W_LOAD_SLOTS - 1):
            copy(c).start()
        for c in range(W_LOAD_CHUNKS):
            if c + W_LOAD_SLOTS - 1 < W_LOAD_CHUNKS:
                copy(c + W_LOAD_SLOTS - 1).start()
            copy(c).wait()
            dst[c * rows:(c + 1) * rows, :] = stage[c % W_LOAD_SLOTS].astype(BF16)

    @pl.when(i == 0)
    def _():
        load_weight(w1_hbm, w1_ref, st1_ref, sem1)
        load_weight(w2_hbm, w2_ref, st2_ref, sem2)
        xn_ref[0] = normalized(x_ref[...], mod_ref)

    def between(s):
        r0, n = _piece(tm, s)
        xn_ref[1 - slot, r0:r0 + n, :] = normalized(tile_rows(i + 1, xnext_ref, r0, n), modn_ref)
        if s < n_cast:
            cast_out[s][...] = cast_in[s][...].astype(BF16)

    acc = _swiglu_chunks(xn_ref, slot, w1_ref, w2_ref, between)
    o_ref[...] = tile_rows(i, x_ref, 0, tm) + 0.5 * _mod_slice(mod_ref, 2) * acc


def _ffn1_call(xr, ctxr, mods3, g, w1, w2, *, rows_per_mod, ctx_mod_row, cast_weights):
    tm = FFN_ROW_TILE
    n_x, n_c = xr.shape[0] // tm, ctxr.shape[0] // tm
    assert n_x * tm == xr.shape[0] and n_c * tm == ctxr.shape[0] and rows_per_mod % tm == 0
    n_tiles = n_x + n_c
    tpb = rows_per_mod // tm
    lat = lambda j: jnp.minimum(j, n_x - 1)
    nxt = lambda i: jnp.minimum(i + 1, n_tiles - 1)
    mod_row = lambda j: jnp.where(j < n_x, j // tpb, ctx_mod_row)
    cast_specs, cast_shapes = [], []
    for w in cast_weights:
        per = 1 if (w.shape[0] // n_x) % BF16_ROWS == 0 else 2
        slab = w.shape[0] * per // n_x
        assert slab * n_x == w.shape[0] * per and slab % BF16_ROWS == 0, (w.shape, n_x)
        cast_specs.append(pl.BlockSpec((slab, w.shape[1]), lambda i, per=per: (lat(i) // per, 0)))
        cast_shapes.append(jax.ShapeDtypeStruct(w.shape, BF16))
    st1_rows, st2_rows = w1.shape[0] // W_LOAD_CHUNKS, w2.shape[0] // W_LOAD_CHUNKS
    assert st1_rows % BF16_ROWS == 0 and st2_rows % BF16_ROWS == 0
    in_specs = [
        pl.BlockSpec((tm, D_MODEL), lambda i: (lat(i), 0)),
        pl.BlockSpec((tm, D_MODEL), lambda i: (lat(i + 1), 0)),
        pl.BlockSpec(ctxr.shape, lambda i: (0, 0), pipeline_mode=pl.Buffered(1)),
        pl.BlockSpec((1, 1, N_MOD * D_MODEL), lambda i: (mod_row(i), 0, 0)),
        pl.BlockSpec((1, 1, N_MOD * D_MODEL), lambda i: (mod_row(nxt(i)), 0, 0)),
        pl.BlockSpec((1, D_MODEL), lambda i: (0, 0)),
        pl.BlockSpec(memory_space=pl.ANY),
        pl.BlockSpec(memory_space=pl.ANY),
    ]
    return pl.pallas_call(
        functools.partial(_ffn1_kernel, n_cast=len(cast_weights), n_x_tiles=n_x, n_c_tiles=n_c),
        grid=(n_tiles,),
        in_specs=in_specs + cast_specs,
        out_specs=[pl.BlockSpec((tm, D_MODEL), lambda i: (i, 0))] + cast_specs,
        out_shape=[jax.ShapeDtypeStruct((n_tiles * tm, D_MODEL), F32)] + cast_shapes,
        scratch_shapes=[pltpu.VMEM((2, tm, D_MODEL), BF16),
                        pltpu.VMEM(w1.shape, BF16), pltpu.VMEM(w2.shape, BF16),
                        pltpu.VMEM((W_LOAD_SLOTS, st1_rows, w1.shape[1]), F32),
                        pltpu.VMEM((W_LOAD_SLOTS, st2_rows, w2.shape[1]), F32),
                        pltpu.SemaphoreType.DMA((W_LOAD_SLOTS,)),
                        pltpu.SemaphoreType.DMA((W_LOAD_SLOTS,))],
        compiler_params=pltpu.CompilerParams(
            dimension_semantics=("arbitrary",), vmem_limit_bytes=VMEM_LIMIT),
        name="ffn1",
    )(xr, xr, ctxr, mods3, mods3, g, w1, w2, *cast_weights)


def _ffn2_kernel(x_ref, xnext_ref, y_ref, ynext_ref, mod_ref, modn_ref, g_ref, w1_ref, w2_ref, fg_ref,
                 o_ref, xn_ref, x2_ref):
    i = pl.program_id(0)
    slot = i % 2
    tm = o_ref.shape[0]

    def x2(xr, yr, m_ref, rows=slice(None)):
        return xr[rows, :] + _mod_slice(m_ref, 5) * yr[rows, :].astype(F32)

    def normalized(x, m_ref):
        y = _rmsnorm(x, g_ref[...])
        return (y * (1 + _mod_slice(m_ref, 7)) + _mod_slice(m_ref, 6)).astype(BF16)

    @pl.when(i == 0)
    def _():
        t = x2(x_ref, y_ref, mod_ref)
        x2_ref[0] = t
        xn_ref[0] = normalized(t, mod_ref)

    def between(s):
        r0, n = _piece(tm, s)
        t = x2(xnext_ref, ynext_ref, modn_ref, slice(r0, r0 + n))
        x2_ref[1 - slot, r0:r0 + n, :] = t
        xn_ref[1 - slot, r0:r0 + n, :] = normalized(t, modn_ref)

    acc = _swiglu_chunks(xn_ref, slot, w1_ref, w2_ref, between)
    out = x2_ref[slot] + 0.5 * _mod_slice(mod_ref, 8) * acc
    o_ref[...] = _rmsnorm(out, fg_ref[...])


def _ffn2_call(xr, y, mods3, g, w1, w2, final_g, *, rows, rows_per_mod):
    tm = FFN_ROW_TILE
    n_tiles = rows // tm
    assert n_tiles * tm == rows and rows_per_mod % tm == 0
    tpb = rows_per_mod // tm
    nxt = lambda i: jnp.minimum(i + 1, n_tiles - 1)
    resident = dict(pipeline_mode=pl.Buffered(1))
    cur_spec = pl.BlockSpec((tm, D_MODEL), lambda i: (i, 0))
    first_spec = pl.BlockSpec((tm, D_MODEL), lambda i: (0, 0), **resident)
    nxt_spec = pl.BlockSpec((tm, D_MODEL), lambda i: (nxt(i), 0))
    vec_spec = pl.BlockSpec((1, D_MODEL), lambda i: (0, 0))
    return pl.pallas_call(
        _ffn2_kernel,
        grid=(n_tiles,),
        in_specs=[
            first_spec, nxt_spec, first_spec, nxt_spec,
            pl.BlockSpec((1, 1, N_MOD * D_MODEL), lambda i: (i // tpb, 0, 0)),
            pl.BlockSpec((1, 1, N_MOD * D_MODEL), lambda i: (nxt(i) // tpb, 0, 0)),
            vec_spec,
            pl.BlockSpec(w1.shape, lambda i: (0, 0), **resident),
            pl.BlockSpec(w2.shape, lambda i: (0, 0), **resident),
            vec_spec,
        ],
        out_specs=cur_spec,
        out_shape=jax.ShapeDtypeStruct((rows, D_MODEL), F32),
        scratch_shapes=[pltpu.VMEM((2, tm, D_MODEL), BF16), pltpu.VMEM((2, tm, D_MODEL), F32)],
        compiler_params=pltpu.CompilerParams(
            dimension_semantics=("arbitrary",), vmem_limit_bytes=VMEM_LIMIT),
        name="ffn2",
    )(xr, xr, y, y, mods3, mods3, g, w1, w2, final_g)


_PROJ_COL = {"q": 0, "k": 1, "v": 2, "g": 3, "u": 4, "vg": 5}
_PROJ_ORDER = ("vg", "k", "q", "u", "g", "v")


def _proj_kernel(x_ref, mod_ref, g_ref, w_ref, *rest, outs, rope):
    if rope:
        cs_ref, sn_ref = rest[:2]
        rest = rest[2:]
    o_refs = dict(zip(outs, rest))
    y = _rmsnorm(x_ref[...], g_ref[...])
    xn = (y * (1 + _mod_slice(mod_ref, 4)) + _mod_slice(mod_ref, 3)).astype(BF16)
    k_scale = HEAD_DIM ** -0.5
    for name in sorted(outs, key=_PROJ_ORDER.index):
        col = _PROJ_COL[name] * RET_WIDTH
        p = _dot(xn, w_ref[:, col:col + RET_WIDTH])
        o_ref = o_refs[name]
        if name in ("q", "k", "v", "g"):
            for h in range(HEADS):
                t = p[:, h * HEAD_DIM:(h + 1) * HEAD_DIM]
                if name in ("q", "k") and rope:
                    t = t * cs_ref[...] + pltpu.roll(t, HEAD_DIM // 2, 1) * sn_ref[...]
                if name == "k":
                    t = t * k_scale
                if name == "g":
                    t = jax.nn.silu(t)
                if name == "k":
                    for cc in range(t.shape[0] // CHUNK):
                        o_ref[h, cc] = t[cc * CHUNK:(cc + 1) * CHUNK, :].T.astype(BF16)
                else:
                    o_ref[h] = t.astype(BF16)
        elif name == "u":
            o_ref[...] = jax.nn.gelu(p).astype(BF16)
        else:
            gv = jax.nn.gelu(p)
            for gi in range(GROUPS):
                t = gv[:, gi * GROUP_DIM:(gi + 1) * GROUP_DIM]
                mu = jnp.mean(t, axis=-1, keepdims=True)
                d = t - mu
                var = jnp.mean(d * d, axis=-1, keepdims=True)
                o_ref[:, gi * GROUP_DIM:(gi + 1) * GROUP_DIM] = (d * lax.rsqrt(var + EPS)).astype(BF16)


def _proj_call(xr, mods3, g, w_in, *, outs, row0, rows, tiles_per_batch, mod_row0, rope_tabs=None):
    tm = ROW_TILE
    assert row0 % tm == 0 and rows % tm == 0
    tile0 = row0 // tm
    mod_map = lambda i: (mod_row0 + i // tiles_per_batch, 0, 0)
    in_specs = [
        pl.BlockSpec((tm, D_MODEL), lambda i: (tile0 + i, 0)),
        pl.BlockSpec((1, 1, N_MOD * D_MODEL), mod_map),
        pl.BlockSpec((1, D_MODEL), lambda i: (0, 0)),
        pl.BlockSpec(w_in.shape, lambda i: (0, 0)),
    ]
    args = [xr, mods3, g, w_in]
    if rope_tabs is not None:
        tab_map = lambda i: (i % tiles_per_batch, 0)
        in_specs += [pl.BlockSpec((tm, HEAD_DIM), tab_map)] * 2
        args += list(rope_tabs)
    out_specs, out_shape = [], []
    for name in outs:
        if name == "k":
            out_specs.append(pl.BlockSpec((HEADS, tm // CHUNK, HEAD_DIM, CHUNK), lambda i: (0, i, 0, 0)))
            out_shape.append(jax.ShapeDtypeStruct((HEADS, rows // CHUNK, HEAD_DIM, CHUNK), BF16))
        elif name in ("q", "v", "g"):
            out_specs.append(pl.BlockSpec((HEADS, tm, HEAD_DIM), lambda i: (0, i, 0)))
            out_shape.append(jax.ShapeDtypeStruct((HEADS, rows, HEAD_DIM), BF16))
        else:
            out_specs.append(pl.BlockSpec((tm, GMLP_WIDTH), lambda i: (i, 0)))
            out_shape.append(jax.ShapeDtypeStruct((rows, GMLP_WIDTH), BF16))
    return pl.pallas_call(
        functools.partial(_proj_kernel, outs=outs, rope=rope_tabs is not None),
        grid=(rows // tm,),
        in_specs=in_specs,
        out_specs=out_specs,
        out_shape=out_shape,
        compiler_params=pltpu.CompilerParams(
            dimension_semantics=("parallel",), vmem_limit_bytes=VMEM_LIMIT),
        name="proj_x" if rope_tabs is not None else "proj_ctx",
    )(*args)


def _ret_head(h, rd_ref, ktc_ref, vc_ref, q_ref, kt_ref, v_ref, sg_ref, o_ref, u_ref, s_ref,
              *, n_chunks, n_ctx_chunks):
    L = CHUNK
    lg_f = -jnp.exp(rd_ref[0, h, 0:1, :])
    lg_b = -jnp.exp(rd_ref[1, h, 0:1, :])
    ii = lax.broadcasted_iota(jnp.int32, (L, L), 0).astype(F32)
    jj = lax.broadcasted_iota(jnp.int32, (L, L), 1).astype(F32)
    diff = ii - jj
    lower = diff >= 0
    upper = diff <= 0
    mask = (jnp.where(lower, jnp.exp(jnp.where(lower, diff, 0.0) * lg_f), 0.0)
            + jnp.where(upper, jnp.exp(jnp.where(upper, -diff, 0.0) * lg_b), 0.0))
    xi_f = jnp.exp((ii + 1.0) * lg_f)
    ze_f = jnp.exp((L - 1.0 - ii) * lg_f)
    xi_b = jnp.exp((L - ii) * lg_b)
    ze_b = jnp.exp(ii * lg_b)
    cd_f = jnp.exp(L * lg_f)
    cd_b = jnp.exp(L * lg_b)

    def kv_pair(kt, v):
        vf = v.astype(F32)
        vz = jnp.concatenate([(vf * ze_f).astype(BF16), (vf * ze_b).astype(BF16)], axis=1)
        return _dot(kt, vz)

    def chunk_rows(c):
        return pl.ds(pl.multiple_of(c * L, L), L)

    uc = [kv_pair(ktc_ref[h, c], vc_ref[h, c * L:(c + 1) * L, :]) for c in range(n_ctx_chunks)]
    s_f = jnp.zeros((HEAD_DIM, HEAD_DIM), F32)
    for c in range(n_ctx_chunks):
        s_f = cd_f * s_f + uc[c][:, :HEAD_DIM]
    s_b = jnp.zeros((HEAD_DIM, HEAD_DIM), F32)
    for c in reversed(range(n_ctx_chunks)):
        s_b = cd_b * s_b + uc[c][:, HEAD_DIM:]

    def u_body(c, carry):
        u_ref[c] = kv_pair(kt_ref[h, c], v_ref[h, chunk_rows(c), :])
        return carry

    lax.fori_loop(0, n_chunks, u_body, 0, unroll=RET_UNROLL)

    def f_body(c, s):
        s_ref[c, 0:HEAD_DIM, :] = s.astype(BF16)
        return cd_f * s + u_ref[c, :, 0:HEAD_DIM]

    lax.fori_loop(0, n_chunks, f_body, s_f, unroll=RET_UNROLL)

    def b_body(t, s):
        c = n_chunks - 1 - t
        s_ref[c, HEAD_DIM:2 * HEAD_DIM, :] = s.astype(BF16)
        return cd_b * s + u_ref[c, :, HEAD_DIM:2 * HEAD_DIM]

    lax.fori_loop(0, n_chunks, b_body, s_b, unroll=RET_UNROLL)

    def o_body(c, carry):
        rows = chunk_rows(c)
        q = q_ref[h, rows, :]
        qf = q.astype(F32)
        p = _dot(q, kt_ref[h, c])
        lhs = jnp.concatenate(
            [(p * mask).astype(BF16), (qf * xi_f).astype(BF16), (qf * xi_b).astype(BF16)], axis=1)
        rhs = jnp.concatenate([v_ref[h, rows, :], s_ref[c]], axis=0)
        o = _dot(lhs, rhs)
        y = o * lax.rsqrt(jnp.mean(o * o, axis=-1, keepdims=True) + EPS)
        o_ref[h, rows, :] = (y * sg_ref[h, rows, :].astype(F32)).astype(BF16)
        return carry

    lax.fori_loop(0, n_chunks, o_body, 0, unroll=RET_UNROLL)


def _ret_kernel(*refs, n_chunks, n_ctx_chunks):
    for h in range(HEADS):
        _ret_head(h, *refs, n_chunks=n_chunks, n_ctx_chunks=n_ctx_chunks)


def _ret_call(rd, ktc, vc, q, kt, v, sg, *, batch, seq, ctx_len):
    n_chunks = seq // CHUNK
    n_ctx_chunks = ctx_len // CHUNK
    x_spec = pl.BlockSpec((HEADS, seq, HEAD_DIM), lambda b: (0, b, 0))
    kt_spec = pl.BlockSpec((HEADS, n_chunks, HEAD_DIM, CHUNK), lambda b: (0, b, 0, 0))
    vc_spec = pl.BlockSpec((HEADS, ctx_len, HEAD_DIM), lambda b: (0, b, 0))
    ktc_spec = pl.BlockSpec((HEADS, n_ctx_chunks, HEAD_DIM, CHUNK), lambda b: (0, b, 0, 0))
    return pl.pallas_call(
        functools.partial(_ret_kernel, n_chunks=n_chunks, n_ctx_chunks=n_ctx_chunks),
        grid=(batch,),
        in_specs=[pl.BlockSpec(rd.shape, lambda b: (0, 0, 0, 0)),
                  ktc_spec, vc_spec, x_spec, kt_spec, x_spec, x_spec],
        out_specs=x_spec,
        out_shape=jax.ShapeDtypeStruct((HEADS, batch * seq, HEAD_DIM), BF16),
        scratch_shapes=[pltpu.VMEM((n_chunks, HEAD_DIM, 2 * HEAD_DIM), F32),
                        pltpu.VMEM((n_chunks, 2 * HEAD_DIM, HEAD_DIM), BF16)],
        compiler_params=pltpu.CompilerParams(
            dimension_semantics=("parallel",), vmem_limit_bytes=VMEM_LIMIT),
        name="retention",
    )(rd, ktc, vc, q, kt, v, sg)


def _mix_kernel(ret_ref, gu_ref, vn_ref, ws_ref, bs_ref, wo_ref, o_ref, mix_ref):
    tm = o_ref.shape[0]
    for h in range(HEADS):
        mix_ref[:, h * HEAD_DIM:(h + 1) * HEAD_DIM] = ret_ref[h]
    for c in range(tm // CHUNK):
        rows = slice(c * CHUNK, (c + 1) * CHUNK)
        for gi in range(GROUPS):
            cols = slice(gi * GROUP_DIM, (gi + 1) * GROUP_DIM)
            mixed = _dot(ws_ref[gi].astype(BF16), vn_ref[rows, cols]) + bs_ref[gi]
            mix_ref[rows, RET_WIDTH + gi * GROUP_DIM:RET_WIDTH + (gi + 1) * GROUP_DIM] = (
                gu_ref[rows, cols].astype(F32) * mixed).astype(BF16)
    o_ref[...] = _dot(mix_ref[...], wo_ref[...]).astype(BF16)


def _mix_call(ret, gu, vn, ws, bs, w_out):
    rows = gu.shape[0]
    tm = MIX_ROW_TILE
    assert rows % tm == 0
    return pl.pallas_call(
        _mix_kernel,
        grid=(rows // tm,),
        in_specs=[
            pl.BlockSpec((HEADS, tm, HEAD_DIM), lambda i: (0, i, 0)),
            pl.BlockSpec((tm, GMLP_WIDTH), lambda i: (i, 0)),
            pl.BlockSpec((tm, GMLP_WIDTH), lambda i: (i, 0)),
            pl.BlockSpec(ws.shape, lambda i: (0, 0, 0)),
            pl.BlockSpec(bs.shape, lambda i: (0, 0, 0)),
            pl.BlockSpec(w_out.shape, lambda i: (0, 0)),
        ],
        out_specs=pl.BlockSpec((tm, D_MODEL), lambda i: (i, 0)),
        out_shape=jax.ShapeDtypeStruct((rows, D_MODEL), BF16),
        scratch_shapes=[pltpu.VMEM((tm, RET_WIDTH + GMLP_WIDTH), BF16)],
        compiler_params=pltpu.CompilerParams(
            dimension_semantics=("parallel",), vmem_limit_bytes=VMEM_LIMIT),
        name="mix",
    )(ret, gu, vn, ws, bs, w_out)


def _rope_tables(seq):
    rows = seq // GRID_W
    row = np.repeat(np.arange(rows, dtype=np.float64), GRID_W)
    col = np.tile(np.arange(GRID_W, dtype=np.float64), rows)
    n_freq = HEAD_DIM // 4
    freqs = ROPE_BASE ** (-np.arange(n_freq, dtype=np.float64) / n_freq)
    ang = np.concatenate([row[:, None] * freqs, col[:, None] * freqs], axis=-1)
    cos, sin = np.cos(ang), np.sin(ang)
    return (jnp.asarray(np.concatenate([cos, cos], axis=-1), F32),
            jnp.asarray(np.concatenate([-sin, sin], axis=-1), F32))


def kernel(x, c, ctx, c_ctx, w_ada, b_ada, norm_g, ffn1_w1, ffn1_w2, w_in, ret_decay,
           gmlp_ws, gmlp_bs, w_out, ffn2_w1, ffn2_w2, final_g):
    batch, seq, d = x.shape
    ctx_len = ctx.shape[1]
    assert w_ada.shape[0] == 1, "single trunk layer only"
    assert d == D_MODEL and seq % ROW_TILE == 0 and (batch * ctx_len) % ROW_TILE == 0
    assert batch + 1 <= MOD_ROWS
    tiles_per_batch = seq // ROW_TILE
    ctx_rows = batch * ctx_len

    xr = x.reshape(batch * seq, d)
    ctxr = ctx.reshape(ctx_rows, d)
    mods3 = _mods_call(c, c_ctx[None], w_ada[0], b_ada[0][None])

    g0, g1, g2 = norm_g[0, 0][None], norm_g[0, 1][None], norm_g[0, 2][None]
    x1, w_in_b, w_out_b, w1b, w2b = _ffn1_call(
        xr, ctxr, mods3, g0, ffn1_w1[0], ffn1_w2[0], rows_per_mod=seq, ctx_mod_row=batch,
        cast_weights=(w_in[0], w_out[0], ffn2_w1[0], ffn2_w2[0]))

    q, kt, v, sg, gu, vn = _proj_call(
        x1, mods3, g1, w_in_b, outs=("q", "k", "v", "g", "u", "vg"), row0=0, rows=batch * seq,
        tiles_per_batch=tiles_per_batch, mod_row0=0, rope_tabs=_rope_tables(seq))
    ktc, vc = _proj_call(x1, mods3, g1, w_in_b, outs=("k", "v"), row0=batch * seq, rows=ctx_rows,
                        tiles_per_batch=ctx_rows // ROW_TILE, mod_row0=batch)

    rd = jnp.broadcast_to(ret_decay[0].astype(F32)[:, :, None, None], (2, HEADS, 8, HEAD_DIM))
    ret = _ret_call(rd, ktc, vc, q, kt, v, sg, batch=batch, seq=seq, ctx_len=ctx_len)

    bs = gmlp_bs[0][:, :, None]
    y = _mix_call(ret, gu, vn, gmlp_ws[0], bs, w_out_b)
    out = _ffn2_call(x1, y, mods3, g2, w1b, w2b, final_g[None], rows=batch * seq, rows_per_mod=seq)
    return out.reshape(batch, seq, d)
```

```python
import functools

import jax
import jax.numpy as jnp
import numpy as np
from jax import lax
from jax.experimental import pallas as pl
from jax.experimental.pallas import tpu as pltpu

F32 = jnp.float32
BF16 = jnp.bfloat16

D_MODEL = 1024
CHUNK = 128
HEADS = 4
HEAD_DIM = 128
GROUPS = 4
GROUP_DIM = 128
RET_WIDTH = HEADS * HEAD_DIM
GMLP_WIDTH = GROUPS * GROUP_DIM
D_FF = 2816
N_MOD = 9
GRID_W = 64
ROPE_BASE = 10000.0
EPS = 1e-6

ROW_TILE = 1024
MIX_ROW_TILE = 2048
FF_TILE = 256
BF16_ROWS = 16
MOD_ROWS = 8
MODS_K_TILE = 128
RET_UNROLL = 32
VMEM_LIMIT = 56 * 1024 * 1024


def _rmsnorm(x, g):
    return x * lax.rsqrt(jnp.mean(x * x, axis=-1, keepdims=True) + EPS) * g


def _mod_slice(mod_ref, k):
    return mod_ref[0, :, k * D_MODEL:(k + 1) * D_MODEL]


def _dot(a, b):
    return jnp.dot(a, b, preferred_element_type=F32)


def _mods_kernel(c_ref, cctx_ref, w_ref, b_ref, o_ref, s_ref):
    batch = c_ref.shape[0]
    k = pl.program_id(0)
    tk = w_ref.shape[0]

    @pl.when(k == 0)
    def _():
        s_ref[...] = jnp.zeros_like(s_ref)
        for kk in range(s_ref.shape[0]):
            cols = slice(kk * tk, (kk + 1) * tk)
            s_ref[kk, 0:batch, :] = jax.nn.silu(c_ref[:, cols])
            s_ref[kk, batch:batch + 1, :] = jax.nn.silu(cctx_ref[:, cols])
        o_ref[:, 0, :] = jnp.broadcast_to(b_ref[...], (MOD_ROWS, b_ref.shape[1]))

    o_ref[:, 0, :] += _dot(s_ref[k].astype(BF16), w_ref[...].astype(BF16))


def _mods_call(c, c_ctx, w_ada, b_ada):
    d, n_out = w_ada.shape
    tk = MODS_K_TILE
    return pl.pallas_call(
        _mods_kernel,
        grid=(d // tk,),
        in_specs=[
            pl.BlockSpec(c.shape, lambda k: (0, 0)),
            pl.BlockSpec(c_ctx.shape, lambda k: (0, 0)),
            pl.BlockSpec((tk, n_out), lambda k: (k, 0)),
            pl.BlockSpec((1, n_out), lambda k: (0, 0)),
        ],
        out_specs=pl.BlockSpec((MOD_ROWS, 1, n_out), lambda k: (0, 0, 0)),
        out_shape=jax.ShapeDtypeStruct((MOD_ROWS, 1, n_out), F32),
        scratch_shapes=[pltpu.VMEM((d // tk, MOD_ROWS, tk), F32)],
        compiler_params=pltpu.CompilerParams(
            dimension_semantics=("arbitrary",), vmem_limit_bytes=VMEM_LIMIT),
        name="mods",
    )(c, c_ctx, w_ada, b_ada)


N_FF_CHUNKS = D_FF // FF_TILE
FFN_ROW_TILE = 512
W_LOAD_CHUNKS = 16
W_LOAD_SLOTS = 4


def _swiglu_chunks(xn_ref, slot, w_gate, w_up, w_down, between, before=None):
    acc = None
    for s in range(N_FF_CHUNKS):
        if before is not None:
            before(s)
        gate = _dot(xn_ref[slot], w_gate(s))
        up = _dot(xn_ref[slot], w_up(s))
        h = (jax.nn.silu(gate) * up).astype(BF16)
        d = _dot(h, w_down(s))
        acc = d if acc is None else acc + d
        between(s)
    return acc


def _piece(tm, s):
    n = -(-tm // (N_FF_CHUNKS * BF16_ROWS)) * BF16_ROWS
    return min(s * n, tm - n), n


def _ffn1_kernel(x_ref, xnext_ref, c_ref, mod_ref, modn_ref, g_ref, w1_hbm, w2_hbm, *rest,
                 n_cast, n_x_tiles, n_c_tiles):
    rest = list(rest)
    cast_in = [rest.pop(0) for _ in range(n_cast)]
    o_ref = rest.pop(0)
    cast_out = [rest.pop(0) for _ in range(n_cast)]
    xn_ref, w1_ref, w2_ref, st1_ref, st2_ref, sem1, sem2 = rest
    i = pl.program_id(0)
    slot = i % 2
    tm = o_ref.shape[0]

    def tile_rows(j, lat_ref, r0, n):
        c0 = jnp.clip(j - n_x_tiles, 0, n_c_tiles - 1) * tm + r0
        ctx_rows = c_ref[pl.ds(pl.multiple_of(c0, BF16_ROWS), n), :]
        return jnp.where(j < n_x_tiles, lat_ref[r0:r0 + n, :], ctx_rows)

    def normalized(x, m_ref):
        y = _rmsnorm(x, g_ref[...])
        return (y * (1 + _mod_slice(m_ref, 1)) + _mod_slice(m_ref, 0)).astype(BF16)

    def load_weight(hbm, dst, stage, sem):
        rows = dst.shape[0] // W_LOAD_CHUNKS

        def copy(c):
            s = c % W_LOAD_SLOTS
            return pltpu.make_async_copy(hbm.at[pl.ds(c * rows, rows), :], stage.at[s], sem.at[s])

        for c in range(W_LOAD_SLOTS - 1):
            copy(c).start()
        for c in range(W_LOAD_CHUNKS):
            if c + W_LOAD_SLOTS - 1 < W_LOAD_CHUNKS:
                copy(c + W_LOAD_SLOTS - 1).start()
            copy(c).wait()
            dst[c * rows:(c + 1) * rows, :] = stage[c % W_LOAD_SLOTS].astype(BF16)

    @pl.when(i == 0)
    def _():
        load_weight(w1_hbm, w1_ref, st1_ref, sem1)
        load_weight(w2_hbm, w2_ref, st2_ref, sem2)
        xn_ref[0] = normalized(x_ref[...], mod_ref)

    def between(s):
        r0, n = _piece(tm, s)
        xn_ref[1 - slot, r0:r0 + n, :] = normalized(tile_rows(i + 1, xnext_ref, r0, n), modn_ref)
        if s < n_cast:
            if len(cast_out[s].shape) == 3:
                for c in range(cast_out[s].shape[0]):
                    cast_out[s][c] = cast_in[s][:, c * FF_TILE:(c + 1) * FF_TILE].astype(BF16)
            else:
                cast_out[s][...] = cast_in[s][...].astype(BF16)

    acc = _swiglu_chunks(
        xn_ref, slot,
        lambda s: w1_ref[:, s * FF_TILE:(s + 1) * FF_TILE],
        lambda s: w1_ref[:, D_FF + s * FF_TILE:D_FF + (s + 1) * FF_TILE],
        lambda s: w2_ref[s * FF_TILE:(s + 1) * FF_TILE, :], between)
    o_ref[...] = tile_rows(i, x_ref, 0, tm) + 0.5 * _mod_slice(mod_ref, 2) * acc


def _ffn1_call(xr, ctxr, mods3, g, w1, w2, *, rows_per_mod, ctx_mod_row, cast_weights, chunked=()):
    tm = FFN_ROW_TILE
    n_x, n_c = xr.shape[0] // tm, ctxr.shape[0] // tm
    assert n_x * tm == xr.shape[0] and n_c * tm == ctxr.shape[0] and rows_per_mod % tm == 0
    n_tiles = n_x + n_c
    tpb = rows_per_mod // tm
    lat = lambda j: jnp.minimum(j, n_x - 1)
    nxt = lambda i: jnp.minimum(i + 1, n_tiles - 1)
    mod_row = lambda j: jnp.where(j < n_x, j // tpb, ctx_mod_row)
    cast_specs, cast_shapes = [], []
    cast_in_specs = []
    for k, w in enumerate(cast_weights):
        per = 1 if (w.shape[0] // n_x) % BF16_ROWS == 0 else 2
        slab = w.shape[0] * per // n_x
        assert slab * n_x == w.shape[0] * per and slab % BF16_ROWS == 0, (w.shape, n_x)
        cast_in_specs.append(pl.BlockSpec((slab, w.shape[1]), lambda i, per=per: (lat(i) // per, 0)))
        if k in chunked:
            n_col = w.shape[1] // FF_TILE
            cast_specs.append(pl.BlockSpec((n_col, slab, FF_TILE), lambda i, per=per: (0, lat(i) // per, 0)))
            cast_shapes.append(jax.ShapeDtypeStruct((n_col, w.shape[0], FF_TILE), BF16))
        else:
            cast_specs.append(cast_in_specs[-1])
            cast_shapes.append(jax.ShapeDtypeStruct(w.shape, BF16))
    st1_rows, st2_rows = w1.shape[0] // W_LOAD_CHUNKS, w2.shape[0] // W_LOAD_CHUNKS
    assert st1_rows % BF16_ROWS == 0 and st2_rows % BF16_ROWS == 0
    in_specs = [
        pl.BlockSpec((tm, D_MODEL), lambda i: (lat(i), 0)),
        pl.BlockSpec((tm, D_MODEL), lambda i: (lat(i + 1), 0)),
        pl.BlockSpec(ctxr.shape, lambda i: (0, 0), pipeline_mode=pl.Buffered(1)),
        pl.BlockSpec((1, 1, N_MOD * D_MODEL), lambda i: (mod_row(i), 0, 0)),
        pl.BlockSpec((1, 1, N_MOD * D_MODEL), lambda i: (mod_row(nxt(i)), 0, 0)),
        pl.BlockSpec((1, D_MODEL), lambda i: (0, 0)),
        pl.BlockSpec(memory_space=pl.ANY),
        pl.BlockSpec(memory_space=pl.ANY),
    ]
    return pl.pallas_call(
        functools.partial(_ffn1_kernel, n_cast=len(cast_weights), n_x_tiles=n_x, n_c_tiles=n_c),
        grid=(n_tiles,),
        in_specs=in_specs + cast_in_specs,
        out_specs=[pl.BlockSpec((tm, D_MODEL), lambda i: (i, 0))] + cast_specs,
        out_shape=[jax.ShapeDtypeStruct((n_tiles * tm, D_MODEL), F32)] + cast_shapes,
        scratch_shapes=[pltpu.VMEM((2, tm, D_MODEL), BF16),
                        pltpu.VMEM(w1.shape, BF16), pltpu.VMEM(w2.shape, BF16),
                        pltpu.VMEM((W_LOAD_SLOTS, st1_rows, w1.shape[1]), F32),
                        pltpu.VMEM((W_LOAD_SLOTS, st2_rows, w2.shape[1]), F32),
                        pltpu.SemaphoreType.DMA((W_LOAD_SLOTS,)),
                        pltpu.SemaphoreType.DMA((W_LOAD_SLOTS,))],
        compiler_params=pltpu.CompilerParams(
            dimension_semantics=("arbitrary",), vmem_limit_bytes=VMEM_LIMIT),
        name="ffn1",
    )(xr, xr, ctxr, mods3, mods3, g, w1, w2, *cast_weights)


def _ffn2_kernel(x_ref, xnext_ref, y_ref, ynext_ref, mod_ref, modn_ref, g_ref, w1_hbm, w2_hbm, fg_ref,
                 o_ref, xn_ref, x2_ref, w1_ref, w2_ref, sem1, sem2):
    i = pl.program_id(0)
    slot = i % 2
    tm = o_ref.shape[0]
    n = N_FF_CHUNKS

    def x2(xr, yr, m_ref, rows=slice(None)):
        return xr[rows, :] + _mod_slice(m_ref, 5) * yr[rows, :].astype(F32)

    def normalized(x, m_ref):
        y = _rmsnorm(x, g_ref[...])
        return (y * (1 + _mod_slice(m_ref, 7)) + _mod_slice(m_ref, 6)).astype(BF16)

    def w1_copy(k):
        return pltpu.make_async_copy(w1_hbm.at[k], w1_ref.at[k], sem1.at[k])

    def w2_copy(k):
        return pltpu.make_async_copy(w2_hbm.at[k], w2_ref.at[k], sem2.at[k])

    def between(s):
        r0, rows = _piece(tm, s)
        t = x2(xnext_ref, ynext_ref, modn_ref, slice(r0, r0 + rows))
        x2_ref[1 - slot, r0:r0 + rows, :] = t
        xn_ref[1 - slot, r0:r0 + rows, :] = normalized(t, modn_ref)

    def run_tile(wait_weights):
        def before(s):
            w1_copy(s).wait()
            w1_copy(n + s).wait()
            w2_copy(s).wait()

        acc = _swiglu_chunks(xn_ref, slot, lambda s: w1_ref[s], lambda s: w1_ref[n + s],
                             lambda s: w2_ref[s], between, before if wait_weights else None)
        out = x2_ref[slot] + 0.5 * _mod_slice(mod_ref, 8) * acc
        o_ref[...] = _rmsnorm(out, fg_ref[...])

    @pl.when(i == 0)
    def _():
        for s in range(n):
            w1_copy(s).start()
            w1_copy(n + s).start()
            w2_copy(s).start()
        t = x2(x_ref, y_ref, mod_ref)
        x2_ref[0] = t
        xn_ref[0] = normalized(t, mod_ref)
        run_tile(True)

    @pl.when(i > 0)
    def _():
        run_tile(False)


def _ffn2_call(xr, y, mods3, g, w1c, w2c, final_g, *, rows, rows_per_mod):
    tm = FFN_ROW_TILE
    n_tiles = rows // tm
    assert n_tiles * tm == rows and rows_per_mod % tm == 0
    assert w1c.shape == (2 * N_FF_CHUNKS, D_MODEL, FF_TILE) and w2c.shape == (N_FF_CHUNKS, FF_TILE, D_MODEL)
    hbm_spec = pl.BlockSpec(memory_space=pl.ANY)
    tpb = rows_per_mod // tm
    nxt = lambda i: jnp.minimum(i + 1, n_tiles - 1)
    resident = dict(pipeline_mode=pl.Buffered(1))
    cur_spec = pl.BlockSpec((tm, D_MODEL), lambda i: (i, 0))
    first_spec = pl.BlockSpec((tm, D_MODEL), lambda i: (0, 0), **resident)
    nxt_spec = pl.BlockSpec((tm, D_MODEL), lambda i: (nxt(i), 0))
    vec_spec = pl.BlockSpec((1, D_MODEL), lambda i: (0, 0))
    return pl.pallas_call(
        _ffn2_kernel,
        grid=(n_tiles,),
        in_specs=[
            first_spec, nxt_spec, first_spec, nxt_spec,
            pl.BlockSpec((1, 1, N_MOD * D_MODEL), lambda i: (i // tpb, 0, 0)),
            pl.BlockSpec((1, 1, N_MOD * D_MODEL), lambda i: (nxt(i) // tpb, 0, 0)),
            vec_spec, hbm_spec, hbm_spec, vec_spec,
        ],
        out_specs=cur_spec,
        out_shape=jax.ShapeDtypeStruct((rows, D_MODEL), F32),
        scratch_shapes=[pltpu.VMEM((2, tm, D_MODEL), BF16), pltpu.VMEM((2, tm, D_MODEL), F32),
                        pltpu.VMEM(w1c.shape, BF16), pltpu.VMEM(w2c.shape, BF16),
                        pltpu.SemaphoreType.DMA((w1c.shape[0],)),
                        pltpu.SemaphoreType.DMA((w2c.shape[0],))],
        compiler_params=pltpu.CompilerParams(
            dimension_semantics=("arbitrary",), vmem_limit_bytes=VMEM_LIMIT),
        name="ffn2",
    )(xr, xr, y, y, mods3, mods3, g, w1c, w2c, final_g)


_PROJ_COL = {"q": 0, "k": 1, "v": 2, "g": 3, "u": 4, "vg": 5}
_PROJ_ORDER = ("vg", "k", "q", "u", "g", "v")


def _proj_kernel(x_ref, mod_ref, g_ref, w_ref, *rest, outs, rope):
    if rope:
        cs_ref, sn_ref = rest[:2]
        rest = rest[2:]
    o_refs = dict(zip(outs, rest))
    y = _rmsnorm(x_ref[...], g_ref[...])
    xn = (y * (1 + _mod_slice(mod_ref, 4)) + _mod_slice(mod_ref, 3)).astype(BF16)
    k_scale = HEAD_DIM ** -0.5
    for name in sorted(outs, key=_PROJ_ORDER.index):
        col = _PROJ_COL[name] * RET_WIDTH
        p = _dot(xn, w_ref[:, col:col + RET_WIDTH])
        o_ref = o_refs[name]
        if name in ("q", "k", "v", "g"):
            for h in range(HEADS):
                t = p[:, h * HEAD_DIM:(h + 1) * HEAD_DIM]
                if name in ("q", "k") and rope:
                    t = t * cs_ref[...] + pltpu.roll(t, HEAD_DIM // 2, 1) * sn_ref[...]
                if name == "k":
                    t = t * k_scale
                if name == "g":
                    t = jax.nn.silu(t)
                if name == "k":
                    for cc in range(t.shape[0] // CHUNK):
                        o_ref[h, cc] = t[cc * CHUNK:(cc + 1) * CHUNK, :].T.astype(BF16)
                else:
                    o_ref[h] = t.astype(BF16)
        elif name == "u":
            o_ref[...] = jax.nn.gelu(p).astype(BF16)
        else:
            gv = jax.nn.gelu(p)
            for gi in range(GROUPS):
                t = gv[:, gi * GROUP_DIM:(gi + 1) * GROUP_DIM]
                mu = jnp.mean(t, axis=-1, keepdims=True)
                d = t - mu
                var = jnp.mean(d * d, axis=-1, keepdims=True)
                o_ref[:, gi * GROUP_DIM:(gi + 1) * GROUP_DIM] = (d * lax.rsqrt(var + EPS)).astype(BF16)


def _proj_call(xr, mods3, g, w_in, *, outs, row0, rows, tiles_per_batch, mod_row0, rope_tabs=None):
    tm = ROW_TILE
    assert row0 % tm == 0 and rows % tm == 0
    tile0 = row0 // tm
    mod_map = lambda i: (mod_row0 + i // tiles_per_batch, 0, 0)
    in_specs = [
        pl.BlockSpec((tm, D_MODEL), lambda i: (tile0 + i, 0)),
        pl.BlockSpec((1, 1, N_MOD * D_MODEL), mod_map),
        pl.BlockSpec((1, D_MODEL), lambda i: (0, 0)),
        pl.BlockSpec(w_in.shape, lambda i: (0, 0)),
    ]
    args = [xr, mods3, g, w_in]
    if rope_tabs is not None:
        tab_map = lambda i: (i % tiles_per_batch, 0)
        in_specs += [pl.BlockSpec((tm, HEAD_DIM), tab_map)] * 2
        args += list(rope_tabs)
    out_specs, out_shape = [], []
    for name in outs:
        if name == "k":
            out_specs.append(pl.BlockSpec((HEADS, tm // CHUNK, HEAD_DIM, CHUNK), lambda i: (0, i, 0, 0)))
            out_shape.append(jax.ShapeDtypeStruct((HEADS, rows // CHUNK, HEAD_DIM, CHUNK), BF16))
        elif name in ("q", "v", "g"):
            out_specs.append(pl.BlockSpec((HEADS, tm, HEAD_DIM), lambda i: (0, i, 0)))
            out_shape.append(jax.ShapeDtypeStruct((HEADS, rows, HEAD_DIM), BF16))
        else:
            out_specs.append(pl.BlockSpec((tm, GMLP_WIDTH), lambda i: (i, 0)))
            out_shape.append(jax.ShapeDtypeStruct((rows, GMLP_WIDTH), BF16))
    return pl.pallas_call(
        functools.partial(_proj_kernel, outs=outs, rope=rope_tabs is not None),
        grid=(rows // tm,),
        in_specs=in_specs,
        out_specs=out_specs,
        out_shape=out_shape,
        compiler_params=pltpu.CompilerParams(
            dimension_semantics=("parallel",), vmem_limit_bytes=VMEM_LIMIT),
        name="proj_x" if rope_tabs is not None else "proj_ctx",
    )(*args)


def _ret_kernel(rd_ref, ktc_ref, vc_ref, q_ref, kt_ref, v_ref, sg_ref, o_ref, u_ref, s_ref,
                *, n_chunks, n_ctx_chunks):
    L = CHUNK
    lg_f = -jnp.exp(rd_ref[0, 0, 0:1, :])
    lg_b = -jnp.exp(rd_ref[1, 0, 0:1, :])
    ii = lax.broadcasted_iota(jnp.int32, (L, L), 0).astype(F32)
    jj = lax.broadcasted_iota(jnp.int32, (L, L), 1).astype(F32)
    diff = ii - jj
    lower = diff >= 0
    upper = diff <= 0
    mask = (jnp.where(lower, jnp.exp(jnp.where(lower, diff, 0.0) * lg_f), 0.0)
            + jnp.where(upper, jnp.exp(jnp.where(upper, -diff, 0.0) * lg_b), 0.0))
    xi_f = jnp.exp((ii + 1.0) * lg_f)
    ze_f = jnp.exp((L - 1.0 - ii) * lg_f)
    xi_b = jnp.exp((L - ii) * lg_b)
    ze_b = jnp.exp(ii * lg_b)
    cd_f = jnp.exp(L * lg_f)
    cd_b = jnp.exp(L * lg_b)

    def kv_pair(kt, v):
        vf = v.astype(F32)
        vz = jnp.concatenate([(vf * ze_f).astype(BF16), (vf * ze_b).astype(BF16)], axis=1)
        return _dot(kt, vz)

    def chunk_rows(c):
        return pl.ds(pl.multiple_of(c * L, L), L)

    uc = [kv_pair(ktc_ref[0, c], vc_ref[0, c * L:(c + 1) * L, :]) for c in range(n_ctx_chunks)]
    s_f = jnp.zeros((HEAD_DIM, HEAD_DIM), F32)
    for c in range(n_ctx_chunks):
        s_f = cd_f * s_f + uc[c][:, :HEAD_DIM]
    s_b = jnp.zeros((HEAD_DIM, HEAD_DIM), F32)
    for c in reversed(range(n_ctx_chunks)):
        s_b = cd_b * s_b + uc[c][:, HEAD_DIM:]

    def u_body(c, carry):
        u_ref[c] = kv_pair(kt_ref[0, c], v_ref[0, chunk_rows(c), :])
        return carry

    lax.fori_loop(0, n_chunks, u_body, 0, unroll=RET_UNROLL)

    def f_body(c, s):
        s_ref[c, 0:HEAD_DIM, :] = s.astype(BF16)
        return cd_f * s + u_ref[c, :, 0:HEAD_DIM]

    lax.fori_loop(0, n_chunks, f_body, s_f, unroll=RET_UNROLL)

    def b_body(t, s):
        c = n_chunks - 1 - t
        s_ref[c, HEAD_DIM:2 * HEAD_DIM, :] = s.astype(BF16)
        return cd_b * s + u_ref[c, :, HEAD_DIM:2 * HEAD_DIM]

    lax.fori_loop(0, n_chunks, b_body, s_b, unroll=RET_UNROLL)

    def o_body(c, carry):
        rows = chunk_rows(c)
        q = q_ref[0, rows, :]
        qf = q.astype(F32)
        p = _dot(q, kt_ref[0, c])
        lhs = jnp.concatenate(
            [(p * mask).astype(BF16), (qf * xi_f).astype(BF16), (qf * xi_b).astype(BF16)], axis=1)
        rhs = jnp.concatenate([v_ref[0, rows, :], s_ref[c]], axis=0)
        o = _dot(lhs, rhs)
        y = o * lax.rsqrt(jnp.mean(o * o, axis=-1, keepdims=True) + EPS)
        o_ref[0, rows, :] = (y * sg_ref[0, rows, :].astype(F32)).astype(BF16)
        return carry

    lax.fori_loop(0, n_chunks, o_body, 0, unroll=RET_UNROLL)


def _ret_call(rd, ktc, vc, q, kt, v, sg, *, batch, seq, ctx_len):
    n_chunks = seq // CHUNK
    n_ctx_chunks = ctx_len // CHUNK
    x_spec = pl.BlockSpec((1, seq, HEAD_DIM), lambda b, h: (h, b, 0))
    kt_spec = pl.BlockSpec((1, n_chunks, HEAD_DIM, CHUNK), lambda b, h: (h, b, 0, 0))
    vc_spec = pl.BlockSpec((1, ctx_len, HEAD_DIM), lambda b, h: (h, b, 0))
    ktc_spec = pl.BlockSpec((1, n_ctx_chunks, HEAD_DIM, CHUNK), lambda b, h: (h, b, 0, 0))
    return pl.pallas_call(
        functools.partial(_ret_kernel, n_chunks=n_chunks, n_ctx_chunks=n_ctx_chunks),
        grid=(batch, HEADS),
        in_specs=[pl.BlockSpec((2, 1, 8, HEAD_DIM), lambda b, h: (0, h, 0, 0)),
                  ktc_spec, vc_spec, x_spec, kt_spec, x_spec, x_spec],
        out_specs=x_spec,
        out_shape=jax.ShapeDtypeStruct((HEADS, batch * seq, HEAD_DIM), BF16),
        scratch_shapes=[pltpu.VMEM((n_chunks, HEAD_DIM, 2 * HEAD_DIM), F32),
                        pltpu.VMEM((n_chunks, 2 * HEAD_DIM, HEAD_DIM), BF16)],
        compiler_params=pltpu.CompilerParams(
            dimension_semantics=("parallel", "parallel"), vmem_limit_bytes=VMEM_LIMIT),
        name="retention",
    )(rd, ktc, vc, q, kt, v, sg)


def _mix_kernel(ret_ref, gu_ref, vn_ref, ws_ref, bs_ref, wo_ref, o_ref, mix_ref):
    tm = o_ref.shape[0]
    for h in range(HEADS):
        mix_ref[:, h * HEAD_DIM:(h + 1) * HEAD_DIM] = ret_ref[h]
    for c in range(tm // CHUNK):
        rows = slice(c * CHUNK, (c + 1) * CHUNK)
        for gi in range(GROUPS):
            cols = slice(gi * GROUP_DIM, (gi + 1) * GROUP_DIM)
            mixed = _dot(ws_ref[gi].astype(BF16), vn_ref[rows, cols]) + bs_ref[gi]
            mix_ref[rows, RET_WIDTH + gi * GROUP_DIM:RET_WIDTH + (gi + 1) * GROUP_DIM] = (
                gu_ref[rows, cols].astype(F32) * mixed).astype(BF16)
    o_ref[...] = _dot(mix_ref[...], wo_ref[...]).astype(BF16)


def _mix_call(ret, gu, vn, ws, bs, w_out):
    rows = gu.shape[0]
    tm = MIX_ROW_TILE
    assert rows % tm == 0
    return pl.pallas_call(
        _mix_kernel,
        grid=(rows // tm,),
        in_specs=[
            pl.BlockSpec((HEADS, tm, HEAD_DIM), lambda i: (0, i, 0)),
            pl.BlockSpec((tm, GMLP_WIDTH), lambda i: (i, 0)),
            pl.BlockSpec((tm, GMLP_WIDTH), lambda i: (i, 0)),
            pl.BlockSpec(ws.shape, lambda i: (0, 0, 0)),
            pl.BlockSpec(bs.shape, lambda i: (0, 0, 0)),
            pl.BlockSpec(w_out.shape, lambda i: (0, 0)),
        ],
        out_specs=pl.BlockSpec((tm, D_MODEL), lambda i: (i, 0)),
        out_shape=jax.ShapeDtypeStruct((rows, D_MODEL), BF16),
        scratch_shapes=[pltpu.VMEM((tm, RET_WIDTH + GMLP_WIDTH), BF16)],
        compiler_params=pltpu.CompilerParams(
            dimension_semantics=("parallel",), vmem_limit_bytes=VMEM_LIMIT),
        name="mix",
    )(ret, gu, vn, ws, bs, w_out)


def _rope_tables(seq):
    rows = seq // GRID_W
    row = np.repeat(np.arange(rows, dtype=np.float64), GRID_W)
    col = np.tile(np.arange(GRID_W, dtype=np.float64), rows)
    n_freq = HEAD_DIM // 4
    freqs = ROPE_BASE ** (-np.arange(n_freq, dtype=np.float64) / n_freq)
    ang = np.concatenate([row[:, None] * freqs, col[:, None] * freqs], axis=-1)
    cos, sin = np.cos(ang), np.sin(ang)
    return (jnp.asarray(np.concatenate([cos, cos], axis=-1), F32),
            jnp.asarray(np.concatenate([-sin, sin], axis=-1), F32))


def kernel(x, c, ctx, c_ctx, w_ada, b_ada, norm_g, ffn1_w1, ffn1_w2, w_in, ret_decay,
           gmlp_ws, gmlp_bs, w_out, ffn2_w1, ffn2_w2, final_g):
    batch, seq, d = x.shape
    ctx_len = ctx.shape[1]
    assert w_ada.shape[0] == 1, "single trunk layer only"
    assert d == D_MODEL and seq % ROW_TILE == 0 and (batch * ctx_len) % ROW_TILE == 0
    assert batch + 1 <= MOD_ROWS
    tiles_per_batch = seq // ROW_TILE
    ctx_rows = batch * ctx_len

    xr = x.reshape(batch * seq, d)
    ctxr = ctx.reshape(ctx_rows, d)
    mods3 = _mods_call(c, c_ctx[None], w_ada[0], b_ada[0][None])

    g0, g1, g2 = norm_g[0, 0][None], norm_g[0, 1][None], norm_g[0, 2][None]
    x1, w_in_b, w_out_b, w1b, w2b = _ffn1_call(
        xr, ctxr, mods3, g0, ffn1_w1[0], ffn1_w2[0], rows_per_mod=seq, ctx_mod_row=batch,
        cast_weights=(w_in[0], w_out[0], ffn2_w1[0], ffn2_w2[0]), chunked=(2,))

    q, kt, v, sg, gu, vn = _proj_call(
        x1, mods3, g1, w_in_b, outs=("q", "k", "v", "g", "u", "vg"), row0=0, rows=batch * seq,
        tiles_per_batch=tiles_per_batch, mod_row0=0, rope_tabs=_rope_tables(seq))
    ktc, vc = _proj_call(x1, mods3, g1, w_in_b, outs=("k", "v"), row0=batch * seq, rows=ctx_rows,
                        tiles_per_batch=ctx_rows // ROW_TILE, mod_row0=batch)

    rd = jnp.broadcast_to(ret_decay[0].astype(F32)[:, :, None, None], (2, HEADS, 8, HEAD_DIM))
    ret = _ret_call(rd, ktc, vc, q, kt, v, sg, batch=batch, seq=seq, ctx_len=ctx_len)

    bs = gmlp_bs[0][:, :, None]
    y = _mix_call(ret, gu, vn, gmlp_ws[0], bs, w_out_b)
    w2c = w2b.reshape(N_FF_CHUNKS, FF_TILE, d)
    out = _ffn2_call(x1, y, mods3, g2, w1b, w2c, final_g[None], rows=batch * seq, rows_per_mod=seq)
    return out.reshape(batch, seq, d)
```
